```python
import jax
import jax.numpy as jnp
from jax import lax
import numpy as np

D_MODEL = 1024
BATCH = 16
SEQ = 2048
DEPTH = 4

N_MIXERS = 3
D_FF = 4 * D_MODEL
N_GROUPS = 4
GROUP_DIM = D_MODEL // N_GROUPS
POOL_WINDOWS = (2, 4, 8, 16)
CONV_WIDTH = 31
CONV_PAD = CONV_WIDTH // 2
D_CONV = D_MODEL
NORM_EPS = 1e-6
LN_EPS = 1e-5
RES_SCALE = (2 * DEPTH) ** -0.5

kernel_name = 'hybrid_pool_conv_fourier_encoder'


def rms_norm(x, g):
    xf = x.astype(jnp.float32)
    y = xf * lax.rsqrt(jnp.mean(xf * xf, axis=-1, keepdims=True) + NORM_EPS)
    return (y * g.astype(jnp.float32)).astype(x.dtype)


def pool_mixer(h, w_in, w_group, scale, w_out):
    b, s, d = h.shape
    u = (h @ w_in).reshape(b, s, N_GROUPS, GROUP_DIM).astype(jnp.float32)
    csum = jnp.pad(jnp.cumsum(u, axis=1), ((0, 0), (1, 0), (0, 0), (0, 0)))
    pos = jnp.arange(s)
    groups = []
    for g, w in enumerate(POOL_WINDOWS):
        lo = jnp.clip(pos - w // 2, 0, s)
        hi = jnp.clip(pos + w // 2, 0, s)
        cnt = (hi - lo).astype(jnp.float32)[None, :, None]
        mean = (csum[:, hi, g] - csum[:, lo, g]) / cnt
        groups.append(mean - u[:, :, g])
    p = jnp.stack(groups, axis=2).astype(h.dtype)
    y = jnp.einsum('bsgc,gcd->bsgd', p, w_group).reshape(b, s, d) * scale
    return y @ w_out


def conv_module(h, w_in, dw, dw_bias, ln_g, ln_b, w_out):
    val, gate = jnp.split(h @ w_in, 2, axis=-1)
    v = val * jax.nn.sigmoid(gate)
    v = lax.conv_general_dilated(
        v, dw[:, None, :].astype(v.dtype), window_strides=(1,),
        padding=((CONV_PAD, CONV_PAD),),
        dimension_numbers=('NWC', 'WIO', 'NWC'),
        feature_group_count=D_CONV) + dw_bias
    vf = v.astype(jnp.float32)
    mu = jnp.mean(vf, axis=-1, keepdims=True)
    var = jnp.mean(jnp.square(vf - mu), axis=-1, keepdims=True)
    vn = (vf - mu) * lax.rsqrt(var + LN_EPS) * ln_g.astype(jnp.float32) + ln_b.astype(jnp.float32)
    return jax.nn.silu(vn).astype(h.dtype) @ w_out


def fourier_mixer(h, w_in, w_out):
    b, s, d = h.shape
    u = (h @ w_in).reshape(b, s, N_GROUPS, GROUP_DIM).astype(jnp.float32)
    f = jnp.fft.fft2(u, axes=(1, 3), norm='ortho').real
    return f.reshape(b, s, d).astype(h.dtype) @ w_out


def sq_relu_mlp(h, w_up, w_down):
    a = jax.nn.relu(h @ w_up)
    return (a * a) @ w_down


def setup_inputs(seed: int = 0) -> dict:
    key = jax.random.key(seed)
    keys = iter(jax.random.split(key, 64))
    f32 = jnp.float32

    def dense(shape, fan_in, scale=1.0):
        return jax.random.normal(next(keys), shape, f32) * (scale * fan_in ** -0.5)

    def gain(n):
        return 1.0 + 0.02 * jax.random.normal(next(keys), (n,), f32)

    def bias(n):
        return 0.02 * jax.random.normal(next(keys), (n,), f32)

    inp = {'x': jax.random.normal(next(keys), (BATCH, SEQ, D_MODEL), f32)}

    def mlp_params(i):
        inp[f'l{i}_norm_mlp'] = gain(D_MODEL)
        inp[f'l{i}_mlp_up'] = dense((D_MODEL, D_FF), D_MODEL)
        inp[f'l{i}_mlp_down'] = dense((D_FF, D_MODEL), D_FF, RES_SCALE)

    def pool_params(i):
        inp[f'l{i}_norm_mix'] = gain(D_MODEL)
        inp[f'l{i}_pool_w_in'] = dense((D_MODEL, D_MODEL), D_MODEL)
        inp[f'l{i}_pool_w_group'] = dense((N_GROUPS, GROUP_DIM, GROUP_DIM), GROUP_DIM)
        inp[f'l{i}_pool_scale'] = gain(D_MODEL)
        inp[f'l{i}_pool_w_out'] = dense((D_MODEL, D_MODEL), D_MODEL, RES_SCALE)
        mlp_params(i)

    def conv_params(i):
        inp[f'l{i}_norm_mix'] = gain(D_MODEL)
        inp[f'l{i}_conv_w_in'] = dense((D_MODEL, 2 * D_CONV), D_MODEL)
        inp[f'l{i}_conv_dw'] = dense((CONV_WIDTH, D_CONV), CONV_WIDTH)
        inp[f'l{i}_conv_dw_bias'] = bias(D_CONV)
        inp[f'l{i}_conv_ln_g'] = gain(D_CONV)
        inp[f'l{i}_conv_ln_b'] = bias(D_CONV)
        inp[f'l{i}_conv_w_out'] = dense((D_CONV, D_MODEL), D_CONV, RES_SCALE)
        mlp_params(i)

    def fourier_params(i):
        inp[f'l{i}_norm_mix'] = gain(D_MODEL)
        inp[f'l{i}_fourier_w_in'] = dense((D_MODEL, D_MODEL), D_MODEL)
        inp[f'l{i}_fourier_w_out'] = dense((D_MODEL, D_MODEL), D_MODEL, RES_SCALE)
        mlp_params(i)

    pool_params(0)
    conv_params(1)
    fourier_params(2)
    pool_params(3)
    inp['final_norm'] = gain(D_MODEL)
    return inp


def reference(x,
              l0_norm_mix, l0_pool_w_in, l0_pool_w_group, l0_pool_scale, l0_pool_w_out,
              l0_norm_mlp, l0_mlp_up, l0_mlp_down,
              l1_norm_mix, l1_conv_w_in, l1_conv_dw, l1_conv_dw_bias, l1_conv_ln_g, l1_conv_ln_b,
              l1_conv_w_out, l1_norm_mlp, l1_mlp_up, l1_mlp_down,
              l2_norm_mix, l2_fourier_w_in, l2_fourier_w_out,
              l2_norm_mlp, l2_mlp_up, l2_mlp_down,
              l3_norm_mix, l3_pool_w_in, l3_pool_w_group, l3_pool_scale, l3_pool_w_out,
              l3_norm_mlp, l3_mlp_up, l3_mlp_down,
              final_norm):
    mixers = (
        lambda h: pool_mixer(h, l0_pool_w_in, l0_pool_w_group, l0_pool_scale, l0_pool_w_out),
        lambda h: conv_module(h, l1_conv_w_in, l1_conv_dw, l1_conv_dw_bias, l1_conv_ln_g,
                              l1_conv_ln_b, l1_conv_w_out),
        lambda h: fourier_mixer(h, l2_fourier_w_in, l2_fourier_w_out),
        lambda h: pool_mixer(h, l3_pool_w_in, l3_pool_w_group, l3_pool_scale, l3_pool_w_out),
    )
    norm_mix = (l0_norm_mix, l1_norm_mix, l2_norm_mix, l3_norm_mix)
    norm_mlp = (l0_norm_mlp, l1_norm_mlp, l2_norm_mlp, l3_norm_mlp)
    mlp_up = (l0_mlp_up, l1_mlp_up, l2_mlp_up, l3_mlp_up)
    mlp_down = (l0_mlp_down, l1_mlp_down, l2_mlp_down, l3_mlp_down)
    for i in range(DEPTH):
        x = x + mixers[i](rms_norm(x, norm_mix[i]))
        x = x + sq_relu_mlp(rms_norm(x, norm_mlp[i]), mlp_up[i], mlp_down[i])
    return rms_norm(x, final_norm)
```

```python
import functools

import numpy as np
import jax
import jax.numpy as jnp
from jax import lax
from jax.experimental import pallas as pl
from jax.experimental.pallas import tpu as pltpu

D_MODEL = 1024
D_FF = 4 * D_MODEL
N_GROUPS = 4
GROUP_DIM = D_MODEL // N_GROUPS
POOL_WINDOWS = (2, 4, 8, 16)
CONV_WIDTH = 31
CONV_PAD = CONV_WIDTH // 2
NORM_EPS = 1e-6
LN_EPS = 1e-5

F32 = jnp.float32
BF16 = jnp.bfloat16

ROW_TILE = 512
FF_CHUNK = 1024
POOL_HALO = 8
CONV_HALO = 16
CONV_ROWS = 64
LANES = 128
VMEM_LIMIT = 56 * 1024 * 1024


def _rms(x, g):
    ms = jnp.mean(x * x, axis=-1, keepdims=True)
    return x * lax.rsqrt(ms + NORM_EPS) * g


def _const_spec(shape):
    n = len(shape)
    return pl.BlockSpec(shape, lambda *_: (0,) * n, pipeline_mode=pl.Buffered(1))


def _params(n_axes):
    return pltpu.CompilerParams(dimension_semantics=("parallel",) * n_axes,
                                vmem_limit_bytes=VMEM_LIMIT)


def _mlp_kernel(x_ref, g_ref, wup_ref, wdn_ref, *rest, final):
    if final:
        gf_ref, o_ref = rest
    else:
        (o_ref,) = rest
    x = x_ref[...]
    h = _rms(x, g_ref[...]).astype(BF16)
    acc = x
    for c in range(D_FF // FF_CHUNK):
        cols = slice(c * FF_CHUNK, (c + 1) * FF_CHUNK)
        a = jnp.dot(h, wup_ref[:, cols], preferred_element_type=F32)
        a = jnp.maximum(a, 0.0)
        a = (a * a).astype(BF16)
        acc = acc + jnp.dot(a, wdn_ref[cols, :], preferred_element_type=F32)
    if final:
        acc = _rms(acc, gf_ref[...])
    o_ref[...] = acc


def _mlp(x2, g, w_up, w_down, final_g=None):
    n = x2.shape[0]
    final = final_g is not None
    row = pl.BlockSpec((ROW_TILE, D_MODEL), lambda i: (i, 0))
    in_specs = [row, _const_spec((1, D_MODEL)), _const_spec((D_MODEL, D_FF)),
                _const_spec((D_FF, D_MODEL))]
    args = [x2, g, w_up, w_down]
    if final:
        in_specs.append(_const_spec((1, D_MODEL)))
        args.append(final_g)
    return pl.pallas_call(
        functools.partial(_mlp_kernel, final=final),
        grid=(n // ROW_TILE,),
        in_specs=in_specs,
        out_specs=row,
        out_shape=jax.ShapeDtypeStruct((n, D_MODEL), F32),
        compiler_params=_params(1),
        name="mlp_final" if final else "mlp",
    )(*args)


def _halo_specs(seq, halo):
    per_tile = ROW_TILE // halo
    last = seq // halo - 1
    cur = pl.BlockSpec((None, ROW_TILE, D_MODEL), lambda b, i: (b, i, 0))
    prev = pl.BlockSpec((None, halo, D_MODEL),
                        lambda b, i: (b, jnp.maximum(i * per_tile - 1, 0), 0))
    nxt = pl.BlockSpec((None, halo, D_MODEL),
                       lambda b, i: (b, jnp.minimum((i + 1) * per_tile, last), 0))
    return prev, cur, nxt


def _ext_positions(halo, rows):
    start = pl.program_id(1) * ROW_TILE - halo
    return start + lax.broadcasted_iota(jnp.int32, (rows, 1), 0)


def _pool_kernel(xp_ref, x_ref, xn_ref, g_ref, win_ref, wg_ref, sc_ref, wout_ref, o_ref,
                 u_ref, *, seq):
    rows = ROW_TILE + 2 * POOL_HALO
    x = x_ref[...]
    g = g_ref[...]
    h_ext = jnp.concatenate([_rms(xp_ref[...], g), _rms(x, g), _rms(xn_ref[...], g)], axis=0)
    u_ext = jnp.dot(h_ext.astype(BF16), win_ref[...], preferred_element_type=F32)
    pos_ext = _ext_positions(POOL_HALO, rows)
    valid = (pos_ext >= 0) & (pos_ext < seq)
    u_ref[...] = jnp.where(valid, u_ext, 0.0)

    pos = pos_ext[POOL_HALO:POOL_HALO + ROW_TILE]
    y = []
    for gi, w in enumerate(POOL_WINDOWS):
        cols = slice(gi * GROUP_DIM, (gi + 1) * GROUP_DIM)
        half = w // 2
        tot = u_ref[pl.ds(POOL_HALO - half, ROW_TILE), cols]
        for k in range(1, w):
            tot = tot + u_ref[pl.ds(POOL_HALO - half + k, ROW_TILE), cols]
        cnt = jnp.minimum(pos + half, seq) - jnp.maximum(pos - half, 0)
        p = tot / cnt.astype(F32) - u_ref[pl.ds(POOL_HALO, ROW_TILE), cols]
        y.append(jnp.dot(p.astype(BF16), wg_ref[gi], preferred_element_type=F32))
    y = jnp.concatenate(y, axis=1) * sc_ref[...]
    o_ref[...] = x + jnp.dot(y.astype(BF16), wout_ref[...], preferred_element_type=F32)


def _pool_mixer(x, g, w_in, w_group, scale, w_out):
    b, s, d = x.shape
    prev, cur, nxt = _halo_specs(s, POOL_HALO)
    return pl.pallas_call(
        functools.partial(_pool_kernel, seq=s),
        grid=(b, s // ROW_TILE),
        in_specs=[prev, cur, nxt, _const_spec((1, d)), _const_spec((d, d)),
                  _const_spec((N_GROUPS, GROUP_DIM, GROUP_DIM)), _const_spec((1, d)),
                  _const_spec((d, d))],
        out_specs=cur,
        out_shape=jax.ShapeDtypeStruct((b, s, d), F32),
        scratch_shapes=[pltpu.VMEM((ROW_TILE + 2 * POOL_HALO, d), F32)],
        compiler_params=_params(2),
        name="pool_mixer",
    )(x, x, x, g, w_in, w_group, scale, w_out)


def _conv_kernel(xp_ref, x_ref, xn_ref, g_ref, win_ref, dw_ref, dwb_ref, lng_ref, lnb_ref,
                 wout_ref, o_ref, v_ref, c_ref, *, seq):
    rows = ROW_TILE + 2 * CONV_HALO
    x = x_ref[...]
    g = g_ref[...]
    h_ext = jnp.concatenate([_rms(xp_ref[...], g), _rms(x, g), _rms(xn_ref[...], g)], axis=0)
    vg = jnp.dot(h_ext.astype(BF16), win_ref[...], preferred_element_type=F32)
    v = vg[:, :D_MODEL] * jax.nn.sigmoid(vg[:, D_MODEL:])
    pos_ext = _ext_positions(CONV_HALO, rows)
    valid = (pos_ext >= 0) & (pos_ext < seq)
    v_ref[...] = jnp.where(valid, v, 0.0)

    shift = CONV_HALO - CONV_PAD
    n_row_blocks = ROW_TILE // CONV_ROWS

    def block(cb, carry):
        cols = pl.ds(pl.multiple_of(cb * LANES, LANES), LANES)
        taps = [dw_ref[pl.ds(k, 1), cols] for k in range(CONV_WIDTH)]
        for rb in range(n_row_blocks):
            r0 = rb * CONV_ROWS
            acc = taps[0] * v_ref[pl.ds(r0 + shift, CONV_ROWS), cols]
            for k in range(1, CONV_WIDTH):
                acc = acc + taps[k] * v_ref[pl.ds(r0 + k + shift, CONV_ROWS), cols]
            c_ref[pl.ds(r0, CONV_ROWS), cols] = acc
        return carry

    lax.fori_loop(0, D_MODEL // LANES, block, 0)

    c = c_ref[...] + dwb_ref[...]
    mu = jnp.mean(c, axis=-1, keepdims=True)
    cc = c - mu
    var = jnp.mean(cc * cc, axis=-1, keepdims=True)
    vn = cc * lax.rsqrt(var + LN_EPS) * lng_ref[...] + lnb_ref[...]
    act = (vn * jax.nn.sigmoid(vn)).astype(BF16)
    o_ref[...] = x + jnp.dot(act, wout_ref[...], preferred_element_type=F32)


def _conv_module(x, g, w_in, dw, dw_bias, ln_g, ln_b, w_out):
    b, s, d = x.shape
    prev, cur, nxt = _halo_specs(s, CONV_HALO)
    return pl.pallas_call(
        functools.partial(_conv_kernel, seq=s),
        grid=(b, s // ROW_TILE),
        in_specs=[prev, cur, nxt, _const_spec((1, d)), _const_spec((d, 2 * d)),
                  _const_spec((CONV_WIDTH, d)), _const_spec((1, d)), _const_spec((1, d)),
                  _const_spec((1, d)), _const_spec((d, d))],
        out_specs=cur,
        out_shape=jax.ShapeDtypeStruct((b, s, d), F32),
        scratch_shapes=[pltpu.VMEM((ROW_TILE + 2 * CONV_HALO, d), F32),
                        pltpu.VMEM((ROW_TILE, d), F32)],
        compiler_params=_params(2),
        name="conv_module",
    )(x, x, x, g, w_in, dw, dw_bias, ln_g, ln_b, w_out)


def _dft_tables(seq):
    def cos_sin(n):
        k = np.arange(n, dtype=np.int64)
        ang = 2.0 * np.pi * ((k[:, None] * k[None, :]) % n).astype(np.float64) / n
        return np.cos(ang) / np.sqrt(n), np.sin(ang) / np.sqrt(n)

    cs, ss = cos_sin(seq)
    cc, sc = cos_sin(GROUP_DIM)
    chan = np.concatenate([cc, sc], axis=1)
    as_bf16 = lambda a: jnp.asarray(a.astype(np.float32)).astype(BF16)
    return as_bf16(cs), as_bf16(-ss), as_bf16(chan)


def _fourier_in_kernel(x_ref, g_ref, win_ref, chan_ref, o_ref):
    h = _rms(x_ref[...], g_ref[...]).astype(BF16)
    u = jnp.dot(h, win_ref[...], preferred_element_type=F32).astype(BF16)
    for gi in range(N_GROUPS):
        cols = slice(gi * GROUP_DIM, (gi + 1) * GROUP_DIM)
        v = jnp.dot(u[:, cols], chan_ref[...], preferred_element_type=F32).astype(BF16)
        o_ref[0, :, cols] = v[:, :GROUP_DIM]
        o_ref[1, :, cols] = v[:, GROUP_DIM:]


def _fourier_out_kernel(cs_ref, nss_ref, v_ref, x_ref, wout_ref, o_ref):
    f = jnp.dot(cs_ref[...], v_ref[0], preferred_element_type=F32)
    f = f + jnp.dot(nss_ref[...], v_ref[1], preferred_element_type=F32)
    o_ref[...] = x_ref[...] + jnp.dot(f.astype(BF16), wout_ref[...], preferred_element_type=F32)


def _fourier_mixer(x, g, w_in, w_out):
    b, s, d = x.shape
    cs, nss, chan = _dft_tables(s)
    tiles = s // ROW_TILE
    v = pl.pallas_call(
        _fourier_in_kernel,
        grid=(b, tiles),
        in_specs=[pl.BlockSpec((None, ROW_TILE, d), lambda bi, i: (bi, i, 0)),
                  _const_spec((1, d)), _const_spec((d, d)),
                  _const_spec((GROUP_DIM, 2 * GROUP_DIM))],
        out_specs=pl.BlockSpec((None, 2, ROW_TILE, d), lambda bi, i: (bi, 0, i, 0)),
        out_shape=jax.ShapeDtypeStruct((b, 2, s, d), BF16),
        compiler_params=_params(2),
        name="fourier_in",
    )(x, g, w_in, chan)
    row = pl.BlockSpec((None, ROW_TILE, d), lambda bi, i: (bi, i, 0))
    table = pl.BlockSpec((ROW_TILE, s), lambda bi, i: (i, 0))
    return pl.pallas_call(
        _fourier_out_kernel,
        grid=(b, tiles),
        in_specs=[table, table,
                  pl.BlockSpec((None, 2, s, d), lambda bi, i: (bi, 0, 0, 0)),
                  row, _const_spec((d, d))],
        out_specs=row,
        out_shape=jax.ShapeDtypeStruct((b, s, d), F32),
        compiler_params=_params(2),
        name="fourier_out",
    )(cs, nss, v, x, w_out)


def kernel(x, l0_norm_mix, l0_pool_w_in, l0_pool_w_group, l0_pool_scale, l0_pool_w_out, l0_norm_mlp, l0_mlp_up, l0_mlp_down, l1_norm_mix, l1_conv_w_in, l1_conv_dw, l1_conv_dw_bias, l1_conv_ln_g, l1_conv_ln_b, l1_conv_w_out, l1_norm_mlp, l1_mlp_up, l1_mlp_down, l2_norm_mix, l2_fourier_w_in, l2_fourier_w_out, l2_norm_mlp, l2_mlp_up, l2_mlp_down, l3_norm_mix, l3_pool_w_in, l3_pool_w_group, l3_pool_scale, l3_pool_w_out, l3_norm_mlp, l3_mlp_up, l3_mlp_down, final_norm):
    b, s, d = x.shape
    row = lambda a: a.astype(F32).reshape(1, -1)
    w = lambda a: a.astype(BF16)

    def mlp(x, g, up, down, final_g=None):
        y = _mlp(x.reshape(b * s, d), row(g), w(up), w(down),
                 None if final_g is None else row(final_g))
        return y.reshape(b, s, d)

    x = _pool_mixer(x, row(l0_norm_mix), w(l0_pool_w_in), w(l0_pool_w_group),
                    row(l0_pool_scale), w(l0_pool_w_out))
    x = mlp(x, l0_norm_mlp, l0_mlp_up, l0_mlp_down)
    x = _conv_module(x, row(l1_norm_mix), w(l1_conv_w_in), l1_conv_dw.astype(F32),
                     row(l1_conv_dw_bias), row(l1_conv_ln_g), row(l1_conv_ln_b),
                     w(l1_conv_w_out))
    x = mlp(x, l1_norm_mlp, l1_mlp_up, l1_mlp_down)
    x = _fourier_mixer(x, row(l2_norm_mix), w(l2_fourier_w_in), w(l2_fourier_w_out))
    x = mlp(x, l2_norm_mlp, l2_mlp_up, l2_mlp_down)
    x = _pool_mixer(x, row(l3_norm_mix), w(l3_pool_w_in), w(l3_pool_w_group),
                    row(l3_pool_scale), w(l3_pool_w_out))
    x = mlp(x, l3_norm_mlp, l3_mlp_up, l3_mlp_down, final_norm)
    return x
```

```python
import functools

import numpy as np
import jax
import jax.numpy as jnp
from jax import lax
from jax.experimental import pallas as pl
from jax.experimental.pallas import tpu as pltpu

D_MODEL = 1024
D_FF = 4 * D_MODEL
N_GROUPS = 4
GROUP_DIM = D_MODEL // N_GROUPS
POOL_WINDOWS = (2, 4, 8, 16)
CONV_WIDTH = 31
CONV_PAD = CONV_WIDTH // 2
NORM_EPS = 1e-6
LN_EPS = 1e-5

F32 = jnp.float32
BF16 = jnp.bfloat16

SUBLANES = 8
LANES = 128
ROW_BLOCKS = D_MODEL // LANES

ROW_TILE = 512
MLP_ROW_TILE = 1024
POOL_ROW_TILE = 1024
FF_CHUNK = 1024
POOL_HALO = 8
CONV_HALO = 16
CONV_EXT = ROW_TILE + 2 * CONV_HALO
CONV_HALF = ROW_TILE // 2
CONV_PAIRS = CONV_HALF + 2 * CONV_HALO
RADIX = 8
BFLY_ROWS = 16
FOURIER_PASS_GROUPS = 2
VMEM_LIMIT = 56 * 1024 * 1024


def _rms(x, g):
    ms = jnp.mean(x * x, axis=-1, keepdims=True)
    return x * lax.rsqrt(ms + NORM_EPS) * g


def _const_spec(shape):
    n = len(shape)
    return pl.BlockSpec(shape, lambda *_: (0,) * n, pipeline_mode=pl.Buffered(1))


def _params(n_axes):
    return pltpu.CompilerParams(dimension_semantics=("parallel",) * n_axes,
                                vmem_limit_bytes=VMEM_LIMIT)


def _mlp_kernel(x_ref, g_ref, wup_ref, wdn_ref, *rest, final):
    if final:
        gf_ref, o_ref = rest
    else:
        (o_ref,) = rest
    x = x_ref[...]
    h = _rms(x, g_ref[...]).astype(BF16)
    acc = x
    for c in range(D_FF // FF_CHUNK):
        cols = slice(c * FF_CHUNK, (c + 1) * FF_CHUNK)
        a = jnp.dot(h, wup_ref[:, cols], preferred_element_type=F32)
        a = jnp.maximum(a, 0.0)
        a = (a * a).astype(BF16)
        acc = acc + jnp.dot(a, wdn_ref[cols, :], preferred_element_type=F32)
    if final:
        acc = _rms(acc, gf_ref[...])
    o_ref[...] = acc


def _mlp(x2, g, w_up, w_down, final_g=None):
    n = x2.shape[0]
    final = final_g is not None
    row = pl.BlockSpec((MLP_ROW_TILE, D_MODEL), lambda i: (i, 0))
    in_specs = [row, _const_spec((1, D_MODEL)), _const_spec((D_MODEL, D_FF)),
                _const_spec((D_FF, D_MODEL))]
    args = [x2, g, w_up, w_down]
    if final:
        in_specs.append(_const_spec((1, D_MODEL)))
        args.append(final_g)
    return pl.pallas_call(
        functools.partial(_mlp_kernel, final=final),
        grid=(n // MLP_ROW_TILE,),
        in_specs=in_specs,
        out_specs=row,
        out_shape=jax.ShapeDtypeStruct((n, D_MODEL), F32),
        compiler_params=_params(1),
        name="mlp_final" if final else "mlp",
    )(*args)


def _halo_specs(seq, tile, halo):
    per_tile = tile // halo
    last = seq // halo - 1
    cur = pl.BlockSpec((None, tile, D_MODEL), lambda b, i: (b, i, 0))
    prev = pl.BlockSpec((None, halo, D_MODEL),
                        lambda b, i: (b, jnp.maximum(i * per_tile - 1, 0), 0))
    nxt = pl.BlockSpec((None, halo, D_MODEL),
                       lambda b, i: (b, jnp.minimum((i + 1) * per_tile, last), 0))
    return prev, cur, nxt


def _ext_positions(tile, halo):
    start = pl.program_id(1) * tile - halo
    return start + lax.broadcasted_iota(jnp.int32, (tile + 2 * halo, 1), 0)


def _window_total(u, w):
    n = u.shape[0]
    fwd = lambda a, k: pltpu.roll(a, n - k, axis=0)
    back = lambda a, k: pltpu.roll(a, k, axis=0)
    if w == 2:
        return u + back(u, 1)
    span = u + fwd(u, 1)
    width = 2
    while 2 * width < w:
        span = span + fwd(span, width)
        width *= 2
    return span + back(span, width)


def _pool_kernel(xp_ref, x_ref, xn_ref, g_ref, win_ref, wg_ref, sc_ref, wout_ref, o_ref, *, seq):
    tile = POOL_ROW_TILE
    x = x_ref[...]
    g = g_ref[...]
    h_ext = jnp.concatenate([_rms(xp_ref[...], g), _rms(x, g), _rms(xn_ref[...], g)], axis=0)
    u_ext = jnp.dot(h_ext.astype(BF16), win_ref[...], preferred_element_type=F32)
    pos_ext = _ext_positions(tile, POOL_HALO)
    valid = (pos_ext >= 0) & (pos_ext < seq)
    u_ext = jnp.where(valid, u_ext, 0.0)

    pos = pos_ext[POOL_HALO:POOL_HALO + tile]
    y = []
    for gi, w in enumerate(POOL_WINDOWS):
        half = w // 2
        u = u_ext[:, gi * GROUP_DIM:(gi + 1) * GROUP_DIM]
        tot = _window_total(u, w)[POOL_HALO:POOL_HALO + tile]
        cnt = jnp.minimum(pos + half, seq) - jnp.maximum(pos - half, 0)
        p = tot / cnt.astype(F32) - u[POOL_HALO:POOL_HALO + tile]
        y.append(jnp.dot(p.astype(BF16), wg_ref[gi], preferred_element_type=F32))
    y = jnp.concatenate(y, axis=1) * sc_ref[...]
    o_ref[...] = x + jnp.dot(y.astype(BF16), wout_ref[...], preferred_element_type=F32)


def _pool_mixer(x, g, w_in, w_group, scale, w_out):
    b, s, d = x.shape
    prev, cur, nxt = _halo_specs(s, POOL_ROW_TILE, POOL_HALO)
    return pl.pallas_call(
        functools.partial(_pool_kernel, seq=s),
        grid=(b, s // POOL_ROW_TILE),
        in_specs=[prev, cur, nxt, _const_spec((1, d)), _const_spec((d, d)),
                  _const_spec((N_GROUPS, GROUP_DIM, GROUP_DIM)), _const_spec((1, d)),
                  _const_spec((d, d))],
        out_specs=cur,
        out_shape=jax.ShapeDtypeStruct((b, s, d), F32),
        compiler_params=_params(2),
        name="pool_mixer",
    )(x, x, x, g, w_in, w_group, scale, w_out)


def _conv_kernel(xp_ref, x_ref, xn_ref, g_ref, win_ref, dwp_ref, dwb_ref, lng_ref, lnb_ref,
                 wout_ref, o_ref, a_ref, bp_ref, d_ref, *, seq):
    x = x_ref[...]
    g = g_ref[...]
    h_ext = jnp.concatenate([_rms(xp_ref[...], g), _rms(x, g), _rms(xn_ref[...], g)], axis=0)
    vg = jnp.dot(h_ext.astype(BF16), win_ref[...], preferred_element_type=F32)
    v = vg[:, :D_MODEL] * jax.nn.sigmoid(vg[:, D_MODEL:])
    pos_ext = _ext_positions(ROW_TILE, CONV_HALO)
    valid = (pos_ext >= 0) & (pos_ext < seq)
    v = jnp.where(valid, v, 0.0)

    for c in range(ROW_BLOCKS):
        a_ref[:, c * SUBLANES:(c + 1) * SUBLANES, :] = (
            v[:, c * LANES:(c + 1) * LANES].reshape(CONV_EXT // SUBLANES, SUBLANES, LANES))

    def row_vreg(t, r):
        return (t, pl.ds(r, ROW_BLOCKS, stride=SUBLANES), slice(None))

    def pack(t, carry):
        for r in range(SUBLANES):
            lo = a_ref[row_vreg(t, r)]
            hi = a_ref[row_vreg(t + CONV_HALF // SUBLANES, r)]
            bp_ref[t * SUBLANES + r] = jnp.concatenate([lo, hi], axis=0).astype(BF16)
        return carry

    lax.fori_loop(0, CONV_PAIRS // SUBLANES, pack, 0)

    shift = CONV_HALO - CONV_PAD

    def conv(t, carry):
        base = t * SUBLANES + shift
        rows = [bp_ref[base + j] for j in range(SUBLANES + CONV_WIDTH - 1)]
        taps = [dwp_ref[k] for k in range(CONV_WIDTH)]
        for r in range(SUBLANES):
            acc = rows[r].astype(F32) * taps[0].astype(F32)
            for k in range(1, CONV_WIDTH):
                acc = acc + rows[r + k].astype(F32) * taps[k].astype(F32)
            d_ref[row_vreg(t, r)] = acc[:ROW_BLOCKS]
            d_ref[row_vreg(t + CONV_HALF // SUBLANES, r)] = acc[ROW_BLOCKS:]
        return carry

    lax.fori_loop(0, CONV_HALF // SUBLANES, conv, 0)

    c = jnp.concatenate(
        [d_ref[:, c * SUBLANES:(c + 1) * SUBLANES, :].reshape(ROW_TILE, LANES) for c in range(ROW_BLOCKS)],
        axis=1) + dwb_ref[...]
    mu = jnp.mean(c, axis=-1, keepdims=True)
    cc = c - mu
    var = jnp.mean(cc * cc, axis=-1, keepdims=True)
    vn = cc * lax.rsqrt(var + LN_EPS) * lng_ref[...] + lnb_ref[...]
    act = (vn * jax.nn.sigmoid(vn)).astype(BF16)
    o_ref[...] = x + jnp.dot(act, wout_ref[...], preferred_element_type=F32)


def _conv_module(x, g, w_in, dw, dw_bias, ln_g, ln_b, w_out):
    b, s, d = x.shape
    prev, cur, nxt = _halo_specs(s, ROW_TILE, CONV_HALO)
    dw_rows = dw.reshape(CONV_WIDTH, ROW_BLOCKS, LANES)
    dwp = jnp.concatenate([dw_rows, dw_rows], axis=1).astype(BF16)
    tiled = lambda rows: pltpu.VMEM((rows // SUBLANES, ROW_BLOCKS * SUBLANES, LANES), F32)
    return pl.pallas_call(
        functools.partial(_conv_kernel, seq=s),
        grid=(b, s // ROW_TILE),
        in_specs=[prev, cur, nxt, _const_spec((1, d)), _const_spec((d, 2 * d)),
                  _const_spec((CONV_WIDTH, 2 * ROW_BLOCKS, LANES)), _const_spec((1, d)),
                  _const_spec((1, d)), _const_spec((1, d)), _const_spec((d, d))],
        out_specs=cur,
        out_shape=jax.ShapeDtypeStruct((b, s, d), F32),
        scratch_shapes=[tiled(CONV_EXT),
                        pltpu.VMEM((CONV_PAIRS, 2 * ROW_BLOCKS, LANES), BF16),
                        tiled(ROW_TILE)],
        compiler_params=_params(2),
        name="conv_module",
    )(x, x, x, g, w_in, dwp, dw_bias, ln_g, ln_b, w_out)


def _dft_tables(seq):
    slab = seq // RADIX
    k1 = np.arange(slab, dtype=np.int64)[:, None]
    s1 = np.arange(slab, dtype=np.int64)[None, :]
    seq_tab = []
    for s2 in range(RADIX):
        ang = 2.0 * np.pi * ((k1 * (RADIX * s1 + s2)) % seq).astype(np.float64) / seq
        seq_tab.append(np.concatenate([np.cos(ang), -np.sin(ang)], axis=0) / np.sqrt(seq))
    c = np.arange(GROUP_DIM, dtype=np.int64)
    ang = 2.0 * np.pi * ((c[:, None] * c[None, :]) % GROUP_DIM).astype(np.float64) / GROUP_DIM
    chan = np.concatenate([np.cos(ang), np.sin(ang)], axis=0) / np.sqrt(GROUP_DIM)
    as_bf16 = lambda a: jnp.asarray(np.asarray(a, dtype=np.float32)).astype(BF16)
    return as_bf16(np.stack(seq_tab)), as_bf16(chan)


def _fourier_in_kernel(x_ref, g_ref, win_ref, o_ref, u_scr):
    h = _rms(x_ref[...], g_ref[...]).astype(BF16)
    u = jnp.dot(h, win_ref[...], preferred_element_type=F32)
    for c in range(ROW_BLOCKS):
        u_scr[c] = u[:, c * LANES:(c + 1) * LANES]
    for s2 in range(RADIX):
        for c in range(ROW_BLOCKS):
            o_ref[s2, :, c * LANES:(c + 1) * LANES] = (
                u_scr[c, pl.ds(s2, ROW_TILE // RADIX, stride=RADIX), :].astype(BF16))


def _butterfly8(a):
    add = lambda p, q: (p[0] + q[0], p[1] + q[1])
    sub = lambda p, q: (p[0] - q[0], p[1] - q[1])
    sub_i = lambda p, q: (p[0] + q[1], p[1] - q[0])
    add_i = lambda p, q: (p[0] - q[1], p[1] + q[0])
    c = np.float32(np.sqrt(0.5))
    w1 = lambda p: ((p[0] + p[1]) * c, (p[1] - p[0]) * c)
    w3 = lambda p: ((p[1] - p[0]) * c, -((p[0] + p[1]) * c))
    p0, p1, p2, p3 = add(a[0], a[4]), sub(a[0], a[4]), add(a[2], a[6]), sub(a[2], a[6])
    q0, q1, q2, q3 = add(a[1], a[5]), sub(a[1], a[5]), add(a[3], a[7]), sub(a[3], a[7])
    e0, e2, e1, e3 = add(p0, p2), sub(p0, p2), sub_i(p1, p3), add_i(p1, p3)
    o0, o2, o1, o3 = add(q0, q2), sub(q0, q2), sub_i(q1, q3), add_i(q1, q3)
    o1, o3 = w1(o1), w3(o3)
    return [add(e0, o0), add(e1, o1), sub_i(e2, o2), add(e3, o3),
            sub(e0, o0), sub(e1, o1), add_i(e2, o2), sub(e3, o3)]


def _fourier_out_kernel(u_ref, tab_ref, chan_ref, x_ref, wout_ref, o_ref, z_scr, xri_scr, f_scr, *, seq):
    slab = seq // RADIX
    chunks = slab // BFLY_ROWS
    pass_cols = FOURIER_PASS_GROUPS * GROUP_DIM

    @pl.when(pl.program_id(1) == 0)
    def _():
        for ps in range(N_GROUPS // FOURIER_PASS_GROUPS):
            cols = slice(ps * pass_cols, (ps + 1) * pass_cols)
            for s2 in range(RADIX):
                z = jnp.dot(tab_ref[s2], u_ref[s2, :, cols], preferred_element_type=F32)
                z_scr[s2] = z.reshape(2, chunks, BFLY_ROWS, pass_cols)

            def bfly(t, carry):
                for lb in range(pass_cols // LANES):
                    lanes = slice(lb * LANES, (lb + 1) * LANES)
                    z = [(z_scr[s2, 0, t, :, lanes], z_scr[s2, 1, t, :, lanes]) for s2 in range(RADIX)]
                    out = _butterfly8(z)
                    gp, off = divmod(lb * LANES, GROUP_DIM)
                    for k2 in range(RADIX):
                        xri_scr[gp, k2, t, :, off:off + LANES] = out[k2][0].astype(BF16)
                        xri_scr[gp, k2, t, :, GROUP_DIM + off:GROUP_DIM + off + LANES] = out[k2][1].astype(BF16)
                return carry

            lax.fori_loop(0, chunks, bfly, 0)
            for gp in range(FOURIER_PASS_GROUPS):
                gi = ps * FOURIER_PASS_GROUPS + gp
                xri = xri_scr[gp].reshape(seq, 2 * GROUP_DIM)
                f = jnp.dot(xri, chan_ref[...], preferred_element_type=F32)
                f_scr[:, gi * GROUP_DIM:(gi + 1) * GROUP_DIM] = f.astype(BF16)

    r0 = pl.multiple_of(pl.program_id(1) * ROW_TILE, ROW_TILE)
    o_ref[...] = x_ref[...] + jnp.dot(f_scr[pl.ds(r0, ROW_TILE), :], wout_ref[...],
                                      preferred_element_type=F32)


def _fourier_mixer(x, g, w_in, w_out):
    b, s, d = x.shape
    slab = s // RADIX
    seq_tab, chan_tab = _dft_tables(s)
    tiles = s // ROW_TILE
    row = pl.BlockSpec((None, ROW_TILE, d), lambda bi, i: (bi, i, 0))
    u = pl.pallas_call(
        _fourier_in_kernel,
        grid=(b, tiles),
        in_specs=[row, _const_spec((1, d)), _const_spec((d, d))],
        out_specs=pl.BlockSpec((None, RADIX, ROW_TILE // RADIX, d), lambda bi, i: (bi, 0, i, 0)),
        out_shape=jax.ShapeDtypeStruct((b, RADIX, slab, d), BF16),
        scratch_shapes=[pltpu.VMEM((ROW_BLOCKS, ROW_TILE, LANES), F32)],
        compiler_params=_params(2),
        name="fourier_in",
    )(x, g, w_in)
    pass_cols = FOURIER_PASS_GROUPS * GROUP_DIM
    return pl.pallas_call(
        functools.partial(_fourier_out_kernel, seq=s),
        grid=(b, tiles),
        in_specs=[pl.BlockSpec((None, RADIX, slab, d), lambda bi, i: (bi, 0, 0, 0)),
                  _const_spec((RADIX, 2 * slab, slab)), _const_spec((2 * GROUP_DIM, GROUP_DIM)),
                  row, _const_spec((d, d))],
        out_specs=row,
        out_shape=jax.ShapeDtypeStruct((b, s, d), F32),
        scratch_shapes=[pltpu.VMEM((RADIX, 2, slab // BFLY_ROWS, BFLY_ROWS, pass_cols), F32),
                        pltpu.VMEM((FOURIER_PASS_GROUPS, RADIX, slab // BFLY_ROWS, BFLY_ROWS, 2 * GROUP_DIM),
                                   BF16),
                        pltpu.VMEM((s, d), BF16)],
        compiler_params=pltpu.CompilerParams(dimension_semantics=("parallel", "arbitrary"),
                                             vmem_limit_bytes=VMEM_LIMIT),
        name="fourier_out",
    )(u, seq_tab, chan_tab, x, w_out)


def kernel(x, l0_norm_mix, l0_pool_w_in, l0_pool_w_group, l0_pool_scale, l0_pool_w_out, l0_norm_mlp, l0_mlp_up, l0_mlp_down, l1_norm_mix, l1_conv_w_in, l1_conv_dw, l1_conv_dw_bias, l1_conv_ln_g, l1_conv_ln_b, l1_conv_w_out, l1_norm_mlp, l1_mlp_up, l1_mlp_down, l2_norm_mix, l2_fourier_w_in, l2_fourier_w_out, l2_norm_mlp, l2_mlp_up, l2_mlp_down, l3_norm_mix, l3_pool_w_in, l3_pool_w_group, l3_pool_scale, l3_pool_w_out, l3_norm_mlp, l3_mlp_up, l3_mlp_down, final_norm):
    b, s, d = x.shape
    row = lambda a: a.astype(F32).reshape(1, -1)
    w = lambda a: a.astype(BF16)

    def mlp(x, g, up, down, final_g=None):
        y = _mlp(x.reshape(b * s, d), row(g), w(up), w(down),
                 None if final_g is None else row(final_g))
        return y.reshape(b, s, d)

    x = _pool_mixer(x, row(l0_norm_mix), w(l0_pool_w_in), w(l0_pool_w_group),
                    row(l0_pool_scale), w(l0_pool_w_out))
    x = mlp(x, l0_norm_mlp, l0_mlp_up, l0_mlp_down)
    x = _conv_module(x, row(l1_norm_mix), w(l1_conv_w_in), l1_conv_dw.astype(F32),
                     row(l1_conv_dw_bias), row(l1_conv_ln_g), row(l1_conv_ln_b),
                     w(l1_conv_w_out))
    x = mlp(x, l1_norm_mlp, l1_mlp_up, l1_mlp_down)
    x = _fourier_mixer(x, row(l2_norm_mix), w(l2_fourier_w_in), w(l2_fourier_w_out))
    x = mlp(x, l2_norm_mlp, l2_mlp_up, l2_mlp_down)
    x = _pool_mixer(x, row(l3_norm_mix), w(l3_pool_w_in), w(l3_pool_w_group),
                    row(l3_pool_scale), w(l3_pool_w_out))
    x = mlp(x, l3_norm_mlp, l3_mlp_up, l3_mlp_down, final_norm)
    return x
```

```python
import functools

import numpy as np
import jax
import jax.numpy as jnp
from jax import lax
from jax.experimental import pallas as pl
from jax.experimental.pallas import tpu as pltpu

D_MODEL = 1024
D_FF = 4 * D_MODEL
N_GROUPS = 4
GROUP_DIM = D_MODEL // N_GROUPS
POOL_WINDOWS = (2, 4, 8, 16)
CONV_WIDTH = 31
CONV_PAD = CONV_WIDTH // 2
NORM_EPS = 1e-6
LN_EPS = 1e-5

F32 = jnp.float32
BF16 = jnp.bfloat16

SUBLANES = 8
LANES = 128
ROW_BLOCKS = D_MODEL // LANES

ROW_TILE = 512
CONV_ROW_TILE = 1024
MLP_ROW_TILE = 1024
POOL_ROW_TILE = 1024
FF_CHUNK = 1024
POOL_HALO = 8
CONV_HALO = 16
CONV_EXT = CONV_ROW_TILE + 2 * CONV_HALO
CONV_HALF = CONV_ROW_TILE // 2
CONV_PAIRS = CONV_HALF + 2 * CONV_HALO
RADIX = 8
BFLY_ROWS = 16
FOURIER_PASS_GROUPS = 2
FOURIER_IN_TILE = 1024
GLU_CHUNKS = 4
GLU_CHUNK_COLS = D_MODEL // GLU_CHUNKS
CONV_OUT_CHUNKS = 2
VMEM_LIMIT = 56 * 1024 * 1024


def _rms(x, g):
    ms = jnp.mean(x * x, axis=-1, keepdims=True)
    return x * lax.rsqrt(ms + NORM_EPS) * g


def _const_spec(shape):
    n = len(shape)
    return pl.BlockSpec(shape, lambda *_: (0,) * n, pipeline_mode=pl.Buffered(1))


def _params(n_axes):
    return pltpu.CompilerParams(dimension_semantics=("parallel",) * n_axes,
                                vmem_limit_bytes=VMEM_LIMIT)


def _cast_plumbing(weights, n_steps, index_map):
    in_specs, out_specs, out_shapes = [], [], []
    for wgt, _ in weights:
        rows, cols = wgt.shape
        block = (rows // n_steps, cols)
        in_specs.append(pl.BlockSpec(block, index_map))
        out_specs.append(pl.BlockSpec(block, index_map))
        out_shapes.append(jax.ShapeDtypeStruct(wgt.shape, BF16))
    return in_specs, out_specs, out_shapes


def _cast_chunks(in_refs, out_refs, glu_flags):
    for i_ref, o_ref, glu in zip(in_refs, out_refs, glu_flags):
        if glu:
            half = i_ref.shape[1] // 2
            for j in range(GLU_CHUNKS):
                src = slice(j * GLU_CHUNK_COLS, (j + 1) * GLU_CHUNK_COLS)
                o_ref[:, 2 * j * GLU_CHUNK_COLS:(2 * j + 1) * GLU_CHUNK_COLS] = i_ref[:, src].astype(BF16)
                o_ref[:, (2 * j + 1) * GLU_CHUNK_COLS:(2 * j + 2) * GLU_CHUNK_COLS] = (
                    i_ref[:, half + j * GLU_CHUNK_COLS:half + (j + 1) * GLU_CHUNK_COLS].astype(BF16))
        else:
            o_ref[...] = i_ref[...].astype(BF16)


def _mlp_kernel(*refs, final, glu_flags):
    n_cast = len(glu_flags)
    n_in = 4 + int(final)
    x_ref, g_ref, wup_ref, wdn_ref = refs[:4]
    cast_in = refs[n_in:n_in + n_cast]
    o_ref = refs[n_in + n_cast]
    cast_out = refs[n_in + n_cast + 1:]
    x = x_ref[...]
    h = _rms(x, g_ref[...]).astype(BF16)
    acc = x
    for c in range(D_FF // FF_CHUNK):
        cols = slice(c * FF_CHUNK, (c + 1) * FF_CHUNK)
        a = jnp.dot(h, wup_ref[:, cols], preferred_element_type=F32)
        a = jnp.maximum(a, 0.0)
        a = (a * a).astype(BF16)
        acc = acc + jnp.dot(a, wdn_ref[cols, :], preferred_element_type=F32)
    if final:
        acc = _rms(acc, refs[4][...])
    o_ref[...] = acc
    _cast_chunks(cast_in, cast_out, glu_flags)


def _mlp(x2, g, w_up, w_down, final_g=None, cast=()):
    n = x2.shape[0]
    final = final_g is not None
    n_steps = n // MLP_ROW_TILE
    row = pl.BlockSpec((MLP_ROW_TILE, D_MODEL), lambda i: (i, 0))
    in_specs = [row, _const_spec((1, D_MODEL)), _const_spec((D_MODEL, D_FF)),
                _const_spec((D_FF, D_MODEL))]
    args = [x2, g, w_up, w_down]
    if final:
        in_specs.append(_const_spec((1, D_MODEL)))
        args.append(final_g)
    c_in, c_out, c_shapes = _cast_plumbing(cast, n_steps, lambda i: (i, 0))
    outs = pl.pallas_call(
        functools.partial(_mlp_kernel, final=final, glu_flags=tuple(glu for _, glu in cast)),
        grid=(n_steps,),
        in_specs=in_specs + c_in,
        out_specs=[row] + c_out,
        out_shape=[jax.ShapeDtypeStruct((n, D_MODEL), F32)] + c_shapes,
        compiler_params=_params(1),
        name="mlp_final" if final else "mlp",
    )(*args, *[wgt for wgt, _ in cast])
    return outs[0], outs[1:]


def _halo_specs(seq, tile, halo):
    per_tile = tile // halo
    last = seq // halo - 1
    cur = pl.BlockSpec((None, tile, D_MODEL), lambda b, i: (b, i, 0))
    prev = pl.BlockSpec((None, halo, D_MODEL),
                        lambda b, i: (b, jnp.maximum(i * per_tile - 1, 0), 0))
    nxt = pl.BlockSpec((None, halo, D_MODEL),
                       lambda b, i: (b, jnp.minimum((i + 1) * per_tile, last), 0))
    return prev, cur, nxt


def _ext_positions(tile, halo):
    start = pl.program_id(1) * tile - halo
    return start + lax.broadcasted_iota(jnp.int32, (tile + 2 * halo, 1), 0)


def _window_total(u, w):
    n = u.shape[0]
    fwd = lambda a, k: pltpu.roll(a, n - k, axis=0)
    back = lambda a, k: pltpu.roll(a, k, axis=0)
    if w == 2:
        return u + back(u, 1)
    span = u + fwd(u, 1)
    width = 2
    while 2 * width < w:
        span = span + fwd(span, width)
        width *= 2
    return span + back(span, width)


def _pool_fold_kernel(wg_ref, sc_ref, wout_ref, o_ref):
    scaled = sc_ref[0] * wout_ref[...]
    o_ref[...] = jnp.dot(wg_ref[0], scaled, preferred_element_type=F32,
                         precision=lax.Precision.HIGHEST).astype(BF16)


def _pool_fold(w_group, scale, w_out):
    d = w_out.shape[1]
    return pl.pallas_call(
        _pool_fold_kernel,
        grid=(N_GROUPS,),
        in_specs=[pl.BlockSpec((1, GROUP_DIM, GROUP_DIM), lambda g: (g, 0, 0)),
                  pl.BlockSpec((1, GROUP_DIM, 1), lambda g: (g, 0, 0)),
                  pl.BlockSpec((GROUP_DIM, d), lambda g: (g, 0))],
        out_specs=pl.BlockSpec((GROUP_DIM, d), lambda g: (g, 0)),
        out_shape=jax.ShapeDtypeStruct((d, d), BF16),
        compiler_params=_params(1),
        name="pool_fold",
    )(w_group.astype(F32), scale.astype(F32).reshape(N_GROUPS, GROUP_DIM, 1), w_out.astype(F32))


def _pool_kernel(xp_ref, x_ref, xn_ref, g_ref, win_ref, weff_ref, *rest, seq, glu_flags):
    n_cast = len(glu_flags)
    cast_in, o_ref, cast_out = rest[:n_cast], rest[n_cast], rest[n_cast + 1:]
    tile = POOL_ROW_TILE
    x = x_ref[...]
    g = g_ref[...]
    h_ext = jnp.concatenate([_rms(xp_ref[...], g), _rms(x, g), _rms(xn_ref[...], g)], axis=0)
    u_ext = jnp.dot(h_ext.astype(BF16), win_ref[...], preferred_element_type=F32)
    pos_ext = _ext_positions(tile, POOL_HALO)
    valid = (pos_ext >= 0) & (pos_ext < seq)
    u_ext = jnp.where(valid, u_ext, 0.0)

    pos = pos_ext[POOL_HALO:POOL_HALO + tile]
    acc = x
    for gi, w in enumerate(POOL_WINDOWS):
        half = w // 2
        grp = slice(gi * GROUP_DIM, (gi + 1) * GROUP_DIM)
        u = u_ext[:, grp]
        tot = _window_total(u, w)[POOL_HALO:POOL_HALO + tile]
        cnt = jnp.minimum(pos + half, seq) - jnp.maximum(pos - half, 0)
        p = tot / cnt.astype(F32) - u[POOL_HALO:POOL_HALO + tile]
        acc = acc + jnp.dot(p.astype(BF16), weff_ref[grp, :], preferred_element_type=F32)
    o_ref[...] = acc
    _cast_chunks(cast_in, cast_out, glu_flags)


def _pool_mixer(x, g, w_in, w_group, scale, w_out, cast=()):
    b, s, d = x.shape
    tiles = s // POOL_ROW_TILE
    prev, cur, nxt = _halo_specs(s, POOL_ROW_TILE, POOL_HALO)
    w_eff = _pool_fold(w_group, scale, w_out)
    c_in, c_out, c_shapes = _cast_plumbing(cast, b * tiles, lambda bi, i: (bi * tiles + i, 0))
    outs = pl.pallas_call(
        functools.partial(_pool_kernel, seq=s, glu_flags=tuple(glu for _, glu in cast)),
        grid=(b, tiles),
        in_specs=[prev, cur, nxt, _const_spec((1, d)), _const_spec((d, d)), _const_spec((d, d))] + c_in,
        out_specs=[cur] + c_out,
        out_shape=[jax.ShapeDtypeStruct((b, s, d), F32)] + c_shapes,
        compiler_params=_params(2),
        name="pool_mixer",
    )(x, x, x, g, w_in, w_eff, *[wgt for wgt, _ in cast])
    return outs[0], outs[1:]


def _conv_kernel(xp_ref, x_ref, xn_ref, g_ref, win_ref, dwp_ref, dwb_ref, lng_ref, lnb_ref,
                 wout_ref, o_ref, a_ref, bp_ref, d_ref, *, seq):
    x = x_ref[...]
    g = g_ref[...]
    h_ext = jnp.concatenate([_rms(xp_ref[...], g), _rms(x, g), _rms(xn_ref[...], g)],
                            axis=0).astype(BF16)
    pos_ext = _ext_positions(CONV_ROW_TILE, CONV_HALO)
    valid = (pos_ext >= 0) & (pos_ext < seq)

    blocks_per_chunk = GLU_CHUNK_COLS // LANES
    for j in range(GLU_CHUNKS):
        vg = jnp.dot(h_ext, win_ref[:, j * 2 * GLU_CHUNK_COLS:(j + 1) * 2 * GLU_CHUNK_COLS],
                     preferred_element_type=F32)
        v = jnp.where(valid, vg[:, :GLU_CHUNK_COLS] * jax.nn.sigmoid(vg[:, GLU_CHUNK_COLS:]), 0.0)
        for cb in range(blocks_per_chunk):
            c = j * blocks_per_chunk + cb
            a_ref[:, c * SUBLANES:(c + 1) * SUBLANES, :] = (
                v[:, cb * LANES:(cb + 1) * LANES].reshape(CONV_EXT // SUBLANES, SUBLANES, LANES))

    def row_vreg(t, r):
        return (t, pl.ds(r, ROW_BLOCKS, stride=SUBLANES), slice(None))

    def pack(t, carry):
        for r in range(SUBLANES):
            lo = a_ref[row_vreg(t, r)]
            hi = a_ref[row_vreg(t + CONV_HALF // SUBLANES, r)]
            bp_ref[t * SUBLANES + r] = jnp.concatenate([lo, hi], axis=0).astype(BF16)
        return carry

    lax.fori_loop(0, CONV_PAIRS // SUBLANES, pack, 0)

    shift = CONV_HALO - CONV_PAD

    def conv(t, carry):
        base = t * SUBLANES + shift
        rows = [bp_ref[base + j] for j in range(SUBLANES + CONV_WIDTH - 1)]
        taps = [dwp_ref[k] for k in range(CONV_WIDTH)]
        for r in range(SUBLANES):
            acc = rows[r].astype(F32) * taps[0].astype(F32)
            for k in range(1, CONV_WIDTH):
                acc = acc + rows[r + k].astype(F32) * taps[k].astype(F32)
            d_ref[row_vreg(t, r)] = acc[:ROW_BLOCKS]
            d_ref[row_vreg(t + CONV_HALF // SUBLANES, r)] = acc[ROW_BLOCKS:]
        return carry

    lax.fori_loop(0, CONV_HALF // SUBLANES, conv, 0)

    rows = CONV_ROW_TILE // CONV_OUT_CHUNKS
    for ch in range(CONV_OUT_CHUNKS):
        tiles = slice(ch * rows // SUBLANES, (ch + 1) * rows // SUBLANES)
        c = jnp.concatenate(
            [d_ref[tiles, c * SUBLANES:(c + 1) * SUBLANES, :].reshape(rows, LANES) for c in range(ROW_BLOCKS)],
            axis=1) + dwb_ref[...]
        mu = jnp.mean(c, axis=-1, keepdims=True)
        cc = c - mu
        var = jnp.mean(cc * cc, axis=-1, keepdims=True)
        vn = cc * lax.rsqrt(var + LN_EPS) * lng_ref[...] + lnb_ref[...]
        act = (vn * jax.nn.sigmoid(vn)).astype(BF16)
        out_rows = slice(ch * rows, (ch + 1) * rows)
        o_ref[out_rows, :] = x_ref[out_rows, :] + jnp.dot(act, wout_ref[...], preferred_element_type=F32)


def _conv_module(x, g, w_glu, dw, dw_bias, ln_g, ln_b, w_out):
    b, s, d = x.shape
    prev, cur, nxt = _halo_specs(s, CONV_ROW_TILE, CONV_HALO)
    dw_rows = dw.reshape(CONV_WIDTH, ROW_BLOCKS, LANES)
    dwp = jnp.concatenate([dw_rows, dw_rows], axis=1).astype(BF16)
    tiled = lambda rows: pltpu.VMEM((rows // SUBLANES, ROW_BLOCKS * SUBLANES, LANES), F32)
    return pl.pallas_call(
        functools.partial(_conv_kernel, seq=s),
        grid=(b, s // CONV_ROW_TILE),
        in_specs=[prev, cur, nxt, _const_spec((1, d)), _const_spec((d, 2 * d)),
                  _const_spec((CONV_WIDTH, 2 * ROW_BLOCKS, LANES)), _const_spec((1, d)),
                  _const_spec((1, d)), _const_spec((1, d)), _const_spec((d, d))],
        out_specs=cur,
        out_shape=jax.ShapeDtypeStruct((b, s, d), F32),
        scratch_shapes=[tiled(CONV_EXT),
                        pltpu.VMEM((CONV_PAIRS, 2 * ROW_BLOCKS, LANES), BF16),
                        tiled(CONV_ROW_TILE)],
        compiler_params=_params(2),
        name="conv_module",
    )(x, x, x, g, w_glu, dwp, dw_bias, ln_g, ln_b, w_out)


def _dft_tables(seq):
    slab = seq // RADIX
    k1 = np.arange(slab, dtype=np.int64)[:, None]
    s1 = np.arange(slab, dtype=np.int64)[None, :]
    seq_tab = []
    for s2 in range(RADIX):
        ang = 2.0 * np.pi * ((k1 * (RADIX * s1 + s2)) % seq).astype(np.float64) / seq
        seq_tab.append(np.concatenate([np.cos(ang), -np.sin(ang)], axis=0) / np.sqrt(seq))
    c = np.arange(GROUP_DIM, dtype=np.int64)
    ang = 2.0 * np.pi * ((c[:, None] * c[None, :]) % GROUP_DIM).astype(np.float64) / GROUP_DIM
    chan = np.concatenate([np.cos(ang), np.sin(ang)], axis=0) / np.sqrt(GROUP_DIM)
    as_bf16 = lambda a: jnp.asarray(np.asarray(a, dtype=np.float32)).astype(BF16)
    return as_bf16(np.stack(seq_tab)), as_bf16(chan)


def _fourier_in_kernel(x_ref, g_ref, win_ref, o_ref, u_scr):
    h = _rms(x_ref[...], g_ref[...]).astype(BF16)
    u = jnp.dot(h, win_ref[...], preferred_element_type=F32)
    for c in range(ROW_BLOCKS):
        u_scr[c] = u[:, c * LANES:(c + 1) * LANES]
    for s2 in range(RADIX):
        for c in range(ROW_BLOCKS):
            o_ref[s2, :, c * LANES:(c + 1) * LANES] = (
                u_scr[c, pl.ds(s2, FOURIER_IN_TILE // RADIX, stride=RADIX), :].astype(BF16))


def _butterfly8(a):
    add = lambda p, q: (p[0] + q[0], p[1] + q[1])
    sub = lambda p, q: (p[0] - q[0], p[1] - q[1])
    sub_i = lambda p, q: (p[0] + q[1], p[1] - q[0])
    add_i = lambda p, q: (p[0] - q[1], p[1] + q[0])
    c = np.float32(np.sqrt(0.5))
    w1 = lambda p: ((p[0] + p[1]) * c, (p[1] - p[0]) * c)
    w3 = lambda p: ((p[1] - p[0]) * c, -((p[0] + p[1]) * c))
    p0, p1, p2, p3 = add(a[0], a[4]), sub(a[0], a[4]), add(a[2], a[6]), sub(a[2], a[6])
    q0, q1, q2, q3 = add(a[1], a[5]), sub(a[1], a[5]), add(a[3], a[7]), sub(a[3], a[7])
    e0, e2, e1, e3 = add(p0, p2), sub(p0, p2), sub_i(p1, p3), add_i(p1, p3)
    o0, o2, o1, o3 = add(q0, q2), sub(q0, q2), sub_i(q1, q3), add_i(q1, q3)
    o1, o3 = w1(o1), w3(o3)
    return [add(e0, o0), add(e1, o1), sub_i(e2, o2), add(e3, o3),
            sub(e0, o0), sub(e1, o1), add_i(e2, o2), sub(e3, o3)]


def _fourier_out_kernel(u_ref, tab_ref, chan_ref, x_ref, wout_ref, o_ref, z_scr, xri_scr, *, seq):
    slab = seq // RADIX
    chunks = slab // BFLY_ROWS
    pass_cols = FOURIER_PASS_GROUPS * GROUP_DIM
    j = pl.program_id(1)

    @pl.when(j == 0)
    def _():
        for ps in range(N_GROUPS // FOURIER_PASS_GROUPS):
            cols = slice(ps * pass_cols, (ps + 1) * pass_cols)
            for s2 in range(RADIX):
                z = jnp.dot(tab_ref[s2], u_ref[s2, :, cols], preferred_element_type=F32)
                z_scr[s2] = z.reshape(2, chunks, BFLY_ROWS, pass_cols)

            def bfly(t, carry):
                for lb in range(pass_cols // LANES):
                    lanes = slice(lb * LANES, (lb + 1) * LANES)
                    z = [(z_scr[s2, 0, t, :, lanes], z_scr[s2, 1, t, :, lanes]) for s2 in range(RADIX)]
                    out = _butterfly8(z)
                    gp, off = divmod(lb * LANES, GROUP_DIM)
                    gi = ps * FOURIER_PASS_GROUPS + gp
                    for k2 in range(RADIX):
                        xri_scr[k2, gi, t, :, off:off + LANES] = out[k2][0].astype(BF16)
                        xri_scr[k2, gi, t, :, GROUP_DIM + off:GROUP_DIM + off + LANES] = out[k2][1].astype(BF16)
                return carry

            lax.fori_loop(0, chunks, bfly, 0)

    slabs_per_tile = ROW_TILE // slab
    f = []
    for gi in range(N_GROUPS):
        xri = xri_scr[pl.ds(j * slabs_per_tile, slabs_per_tile), gi].reshape(ROW_TILE, 2 * GROUP_DIM)
        f.append(jnp.dot(xri, chan_ref[...], preferred_element_type=F32).astype(BF16))
    f = jnp.concatenate(f, axis=1)
    o_ref[...] = x_ref[...] + jnp.dot(f, wout_ref[...], preferred_element_type=F32)


def _fourier_mixer(x, g, w_in, w_out):
    b, s, d = x.shape
    slab = s // RADIX
    seq_tab, chan_tab = _dft_tables(s)
    u = pl.pallas_call(
        _fourier_in_kernel,
        grid=(b, s // FOURIER_IN_TILE),
        in_specs=[pl.BlockSpec((None, FOURIER_IN_TILE, d), lambda bi, i: (bi, i, 0)),
                  _const_spec((1, d)), _const_spec((d, d))],
        out_specs=pl.BlockSpec((None, RADIX, FOURIER_IN_TILE // RADIX, d), lambda bi, i: (bi, 0, i, 0)),
        out_shape=jax.ShapeDtypeStruct((b, RADIX, slab, d), BF16),
        scratch_shapes=[pltpu.VMEM((ROW_BLOCKS, FOURIER_IN_TILE, LANES), F32)],
        compiler_params=_params(2),
        name="fourier_in",
    )(x, g, w_in)
    pass_cols = FOURIER_PASS_GROUPS * GROUP_DIM
    row = pl.BlockSpec((None, ROW_TILE, d), lambda bi, i: (bi, i, 0))
    return pl.pallas_call(
        functools.partial(_fourier_out_kernel, seq=s),
        grid=(b, s // ROW_TILE),
        in_specs=[pl.BlockSpec((None, RADIX, slab, d), lambda bi, i: (bi, 0, 0, 0)),
                  _const_spec((RADIX, 2 * slab, slab)), _const_spec((2 * GROUP_DIM, GROUP_DIM)),
                  row, _const_spec((d, d))],
        out_specs=row,
        out_shape=jax.ShapeDtypeStruct((b, s, d), F32),
        scratch_shapes=[pltpu.VMEM((RADIX, 2, slab // BFLY_ROWS, BFLY_ROWS, pass_cols), F32),
                        pltpu.VMEM((RADIX, N_GROUPS, slab // BFLY_ROWS, BFLY_ROWS, 2 * GROUP_DIM), BF16)],
        compiler_params=pltpu.CompilerParams(dimension_semantics=("parallel", "arbitrary"),
                                             vmem_limit_bytes=VMEM_LIMIT),
        name="fourier_out",
    )(u, seq_tab, chan_tab, x, w_out)


def kernel(x, l0_norm_mix, l0_pool_w_in, l0_pool_w_group, l0_pool_scale, l0_pool_w_out, l0_norm_mlp, l0_mlp_up, l0_mlp_down, l1_norm_mix, l1_conv_w_in, l1_conv_dw, l1_conv_dw_bias, l1_conv_ln_g, l1_conv_ln_b, l1_conv_w_out, l1_norm_mlp, l1_mlp_up, l1_mlp_down, l2_norm_mix, l2_fourier_w_in, l2_fourier_w_out, l2_norm_mlp, l2_mlp_up, l2_mlp_down, l3_norm_mix, l3_pool_w_in, l3_pool_w_group, l3_pool_scale, l3_pool_w_out, l3_norm_mlp, l3_mlp_up, l3_mlp_down, final_norm):
    b, s, d = x.shape
    row = lambda a: a.astype(F32).reshape(1, -1)
    plain = lambda *ws: [(wgt.astype(F32), False) for wgt in ws]

    def mlp(x, g, up, down, final_g=None, cast=()):
        y, cast_out = _mlp(x.reshape(b * s, d), row(g), up, down,
                           None if final_g is None else row(final_g), cast)
        return y.reshape(b, s, d), cast_out

    x, (up0, down0) = _pool_mixer(x, row(l0_norm_mix), l0_pool_w_in.astype(BF16), l0_pool_w_group,
                                  l0_pool_scale, l0_pool_w_out, plain(l0_mlp_up, l0_mlp_down))
    x, (c_in, c_out, up1, down1) = mlp(
        x, l0_norm_mlp, up0, down0,
        cast=[(l1_conv_w_in.astype(F32), True)] + plain(l1_conv_w_out, l1_mlp_up, l1_mlp_down))
    x = _conv_module(x, row(l1_norm_mix), c_in, l1_conv_dw.astype(F32), row(l1_conv_dw_bias),
                     row(l1_conv_ln_g), row(l1_conv_ln_b), c_out)
    x, (f_in, f_out, up2, down2) = mlp(
        x, l1_norm_mlp, up1, down1,
        cast=plain(l2_fourier_w_in, l2_fourier_w_out, l2_mlp_up, l2_mlp_down))
    x = _fourier_mixer(x, row(l2_norm_mix), f_in, f_out)
    x, (p_in, up3, down3) = mlp(x, l2_norm_mlp, up2, down2, cast=plain(l3_pool_w_in, l3_mlp_up, l3_mlp_down))
    x, _ = _pool_mixer(x, row(l3_norm_mix), p_in, l3_pool_w_group, l3_pool_scale, l3_pool_w_out)
    x, _ = mlp(x, l3_norm_mlp, up3, down3, final_norm)
    return x
```

```python
import functools

import numpy as np
import jax
import jax.numpy as jnp
from jax import lax
from jax.experimental import pallas as pl
from jax.experimental.pallas import tpu as pltpu

D_MODEL = 1024
D_FF = 4 * D_MODEL
N_GROUPS = 4
GROUP_DIM = D_MODEL // N_GROUPS
POOL_WINDOWS = (2, 4, 8, 16)
CONV_WIDTH = 31
CONV_PAD = CONV_WIDTH // 2
NORM_EPS = 1e-6
LN_EPS = 1e-5

F32 = jnp.float32
BF16 = jnp.bfloat16

SUBLANES = 8
LANES = 128
ROW_BLOCKS = D_MODEL // LANES

ROW_TILE = 1024
CONV_ROW_TILE = 1024
MLP_ROW_TILE = 1024
POOL_ROW_TILE = 1024
FF_CHUNK = 1024
POOL_HALO = 8
CONV_HALO = 16
CONV_EXT = CONV_ROW_TILE + 2 * CONV_HALO
CONV_HALF = CONV_ROW_TILE // 2
CONV_PAIRS = CONV_HALF + 2 * CONV_HALO
RADIX = 8
BFLY_ROWS = 16
FOURIER_PASS_GROUPS = 2
FOURIER_IN_TILE = 1024
GLU_CHUNKS = 4
GLU_CHUNK_COLS = D_MODEL // GLU_CHUNKS
CONV_OUT_CHUNKS = 2
VMEM_LIMIT = 56 * 1024 * 1024


def _rms(x, g):
    ms = jnp.mean(x * x, axis=-1, keepdims=True)
    return x * lax.rsqrt(ms + NORM_EPS) * g


def _const_spec(shape):
    n = len(shape)
    return pl.BlockSpec(shape, lambda *_: (0,) * n, pipeline_mode=pl.Buffered(1))


def _params(n_axes):
    return pltpu.CompilerParams(dimension_semantics=("parallel",) * n_axes,
                                vmem_limit_bytes=VMEM_LIMIT)


def _cast_plumbing(weights, n_steps, index_map):
    in_specs, out_specs, out_shapes = [], [], []
    for wgt, _ in weights:
        rows, cols = wgt.shape
        block = (rows // n_steps, cols)
        in_specs.append(pl.BlockSpec(block, index_map))
        out_specs.append(pl.BlockSpec(block, index_map))
        out_shapes.append(jax.ShapeDtypeStruct(wgt.shape, BF16))
    return in_specs, out_specs, out_shapes


def _cast_chunks(in_refs, out_refs, glu_flags):
    for i_ref, o_ref, glu in zip(in_refs, out_refs, glu_flags):
        if glu:
            half = i_ref.shape[1] // 2
            for j in range(GLU_CHUNKS):
                src = slice(j * GLU_CHUNK_COLS, (j + 1) * GLU_CHUNK_COLS)
                o_ref[:, 2 * j * GLU_CHUNK_COLS:(2 * j + 1) * GLU_CHUNK_COLS] = i_ref[:, src].astype(BF16)
                o_ref[:, (2 * j + 1) * GLU_CHUNK_COLS:(2 * j + 2) * GLU_CHUNK_COLS] = (
                    i_ref[:, half + j * GLU_CHUNK_COLS:half + (j + 1) * GLU_CHUNK_COLS].astype(BF16))
        else:
            o_ref[...] = i_ref[...].astype(BF16)


def _mlp_kernel(*refs, final, glu_flags):
    n_cast = len(glu_flags)
    n_in = 4 + int(final)
    x_ref, g_ref, wup_ref, wdn_ref = refs[:4]
    cast_in = refs[n_in:n_in + n_cast]
    o_ref = refs[n_in + n_cast]
    cast_out = refs[n_in + n_cast + 1:]
    x = x_ref[...]
    h = _rms(x, g_ref[...]).astype(BF16)
    acc = x
    for c in range(D_FF // FF_CHUNK):
        cols = slice(c * FF_CHUNK, (c + 1) * FF_CHUNK)
        a = jnp.dot(h, wup_ref[:, cols], preferred_element_type=F32)
        a = jnp.maximum(a, 0.0)
        a = (a * a).astype(BF16)
        acc = acc + jnp.dot(a, wdn_ref[cols, :], preferred_element_type=F32)
    if final:
        acc = _rms(acc, refs[4][...])
    o_ref[...] = acc
    _cast_chunks(cast_in, cast_out, glu_flags)


def _mlp(x2, g, w_up, w_down, final_g=None, cast=()):
    n = x2.shape[0]
    final = final_g is not None
    n_steps = n // MLP_ROW_TILE
    row = pl.BlockSpec((MLP_ROW_TILE, D_MODEL), lambda i: (i, 0))
    in_specs = [row, _const_spec((1, D_MODEL)), _const_spec((D_MODEL, D_FF)),
                _const_spec((D_FF, D_MODEL))]
    args = [x2, g, w_up, w_down]
    if final:
        in_specs.append(_const_spec((1, D_MODEL)))
        args.append(final_g)
    c_in, c_out, c_shapes = _cast_plumbing(cast, n_steps, lambda i: (i, 0))
    outs = pl.pallas_call(
        functools.partial(_mlp_kernel, final=final, glu_flags=tuple(glu for _, glu in cast)),
        grid=(n_steps,),
        in_specs=in_specs + c_in,
        out_specs=[row] + c_out,
        out_shape=[jax.ShapeDtypeStruct((n, D_MODEL), F32)] + c_shapes,
        compiler_params=_params(1),
        name="mlp_final" if final else "mlp",
    )(*args, *[wgt for wgt, _ in cast])
    return outs[0], outs[1:]


def _halo_specs(seq, tile, halo):
    per_tile = tile // halo
    last = seq // halo - 1
    cur = pl.BlockSpec((None, tile, D_MODEL), lambda b, i: (b, i, 0))
    prev = pl.BlockSpec((None, halo, D_MODEL),
                        lambda b, i: (b, jnp.maximum(i * per_tile - 1, 0), 0))
    nxt = pl.BlockSpec((None, halo, D_MODEL),
                       lambda b, i: (b, jnp.minimum((i + 1) * per_tile, last), 0))
    return prev, cur, nxt


def _ext_positions(tile, halo):
    start = pl.program_id(1) * tile - halo
    return start + lax.broadcasted_iota(jnp.int32, (tile + 2 * halo, 1), 0)


def _window_total(u, w):
    n = u.shape[0]
    fwd = lambda a, k: pltpu.roll(a, n - k, axis=0)
    back = lambda a, k: pltpu.roll(a, k, axis=0)
    if w == 2:
        return u + back(u, 1)
    span = u + fwd(u, 1)
    width = 2
    while 2 * width < w:
        span = span + fwd(span, width)
        width *= 2
    return span + back(span, width)


def _pool_fold_kernel(wg_ref, sc_ref, wout_ref, o_ref):
    scaled = sc_ref[0] * wout_ref[...]
    o_ref[...] = jnp.dot(wg_ref[0], scaled, preferred_element_type=F32,
                         precision=lax.Precision.HIGHEST).astype(BF16)


def _pool_fold(w_group, scale, w_out):
    d = w_out.shape[1]
    return pl.pallas_call(
        _pool_fold_kernel,
        grid=(N_GROUPS,),
        in_specs=[pl.BlockSpec((1, GROUP_DIM, GROUP_DIM), lambda g: (g, 0, 0)),
                  pl.BlockSpec((1, GROUP_DIM, 1), lambda g: (g, 0, 0)),
                  pl.BlockSpec((GROUP_DIM, d), lambda g: (g, 0))],
        out_specs=pl.BlockSpec((GROUP_DIM, d), lambda g: (g, 0)),
        out_shape=jax.ShapeDtypeStruct((d, d), BF16),
        compiler_params=_params(1),
        name="pool_fold",
    )(w_group.astype(F32), scale.astype(F32).reshape(N_GROUPS, GROUP_DIM, 1), w_out.astype(F32))


def _pool_kernel(xp_ref, x_ref, xn_ref, g_ref, win_ref, weff_ref, *rest, seq, glu_flags):
    n_cast = len(glu_flags)
    cast_in, o_ref, cast_out = rest[:n_cast], rest[n_cast], rest[n_cast + 1:]
    tile = POOL_ROW_TILE
    x = x_ref[...]
    g = g_ref[...]
    h_ext = jnp.concatenate([_rms(xp_ref[...], g), _rms(x, g), _rms(xn_ref[...], g)], axis=0)
    u_ext = jnp.dot(h_ext.astype(BF16), win_ref[...], preferred_element_type=F32)
    pos_ext = _ext_positions(tile, POOL_HALO)
    valid = (pos_ext >= 0) & (pos_ext < seq)
    u_ext = jnp.where(valid, u_ext, 0.0)

    pos = pos_ext[POOL_HALO:POOL_HALO + tile]
    pooled = []
    for gi, w in enumerate(POOL_WINDOWS):
        half = w // 2
        u = u_ext[:, gi * GROUP_DIM:(gi + 1) * GROUP_DIM]
        tot = _window_total(u, w)[POOL_HALO:POOL_HALO + tile]
        cnt = jnp.minimum(pos + half, seq) - jnp.maximum(pos - half, 0)
        p = tot / cnt.astype(F32) - u[POOL_HALO:POOL_HALO + tile]
        pooled.append(p.astype(BF16))
    o_ref[...] = x + jnp.dot(jnp.concatenate(pooled, axis=1), weff_ref[...],
                             preferred_element_type=F32)
    _cast_chunks(cast_in, cast_out, glu_flags)


def _pool_mixer(x, g, w_in, w_group, scale, w_out, cast=()):
    b, s, d = x.shape
    tiles = s // POOL_ROW_TILE
    prev, cur, nxt = _halo_specs(s, POOL_ROW_TILE, POOL_HALO)
    w_eff = _pool_fold(w_group, scale, w_out)
    c_in, c_out, c_shapes = _cast_plumbing(cast, b * tiles, lambda bi, i: (bi * tiles + i, 0))
    outs = pl.pallas_call(
        functools.partial(_pool_kernel, seq=s, glu_flags=tuple(glu for _, glu in cast)),
        grid=(b, tiles),
        in_specs=[prev, cur, nxt, _const_spec((1, d)), _const_spec((d, d)), _const_spec((d, d))] + c_in,
        out_specs=[cur] + c_out,
        out_shape=[jax.ShapeDtypeStruct((b, s, d), F32)] + c_shapes,
        compiler_params=_params(2),
        name="pool_mixer",
    )(x, x, x, g, w_in, w_eff, *[wgt for wgt, _ in cast])
    return outs[0], outs[1:]


def _conv_kernel(xp_ref, x_ref, xn_ref, g_ref, win_ref, dwp_ref, dwb_ref, lng_ref, lnb_ref,
                 wout_ref, o_ref, a_ref, bp_ref, d_ref, *, seq):
    x = x_ref[...]
    g = g_ref[...]
    h_ext = jnp.concatenate([_rms(xp_ref[...], g), _rms(x, g), _rms(xn_ref[...], g)],
                            axis=0).astype(BF16)
    pos_ext = _ext_positions(CONV_ROW_TILE, CONV_HALO)
    valid = (pos_ext >= 0) & (pos_ext < seq)

    blocks_per_chunk = GLU_CHUNK_COLS // LANES
    for j in range(GLU_CHUNKS):
        vg = jnp.dot(h_ext, win_ref[:, j * 2 * GLU_CHUNK_COLS:(j + 1) * 2 * GLU_CHUNK_COLS],
                     preferred_element_type=F32)
        v = jnp.where(valid, vg[:, :GLU_CHUNK_COLS] * jax.nn.sigmoid(vg[:, GLU_CHUNK_COLS:]), 0.0)
        for cb in range(blocks_per_chunk):
            c = j * blocks_per_chunk + cb
            a_ref[:, c * SUBLANES:(c + 1) * SUBLANES, :] = (
                v[:, cb * LANES:(cb + 1) * LANES].reshape(CONV_EXT // SUBLANES, SUBLANES, LANES))

    def row_vreg(t, r):
        return (t, pl.ds(r, ROW_BLOCKS, stride=SUBLANES), slice(None))

    def pack(t, carry):
        for r in range(SUBLANES):
            lo = a_ref[row_vreg(t, r)]
            hi = a_ref[row_vreg(t + CONV_HALF // SUBLANES, r)]
            bp_ref[t * SUBLANES + r] = jnp.concatenate([lo, hi], axis=0).astype(BF16)
        return carry

    lax.fori_loop(0, CONV_PAIRS // SUBLANES, pack, 0)

    shift = CONV_HALO - CONV_PAD

    def conv(t, carry):
        base = t * SUBLANES + shift
        rows = [bp_ref[base + j] for j in range(SUBLANES + CONV_WIDTH - 1)]
        taps = [dwp_ref[k] for k in range(CONV_WIDTH)]
        for r in range(SUBLANES):
            acc = rows[r].astype(F32) * taps[0].astype(F32)
            for k in range(1, CONV_WIDTH):
                acc = acc + rows[r + k].astype(F32) * taps[k].astype(F32)
            d_ref[row_vreg(t, r)] = acc[:ROW_BLOCKS]
            d_ref[row_vreg(t + CONV_HALF // SUBLANES, r)] = acc[ROW_BLOCKS:]
        return carry

    lax.fori_loop(0, CONV_HALF // SUBLANES, conv, 0)

    rows = CONV_ROW_TILE // CONV_OUT_CHUNKS
    for ch in range(CONV_OUT_CHUNKS):
        tiles = slice(ch * rows // SUBLANES, (ch + 1) * rows // SUBLANES)
        c = jnp.concatenate(
            [d_ref[tiles, c * SUBLANES:(c + 1) * SUBLANES, :].reshape(rows, LANES) for c in range(ROW_BLOCKS)],
            axis=1) + dwb_ref[...]
        mu = jnp.mean(c, axis=-1, keepdims=True)
        cc = c - mu
        var = jnp.mean(cc * cc, axis=-1, keepdims=True)
        vn = cc * lax.rsqrt(var + LN_EPS) * lng_ref[...] + lnb_ref[...]
        act = (vn * jax.nn.sigmoid(vn)).astype(BF16)
        out_rows = slice(ch * rows, (ch + 1) * rows)
        o_ref[out_rows, :] = x_ref[out_rows, :] + jnp.dot(act, wout_ref[...], preferred_element_type=F32)


def _conv_module(x, g, w_glu, dw, dw_bias, ln_g, ln_b, w_out):
    b, s, d = x.shape
    prev, cur, nxt = _halo_specs(s, CONV_ROW_TILE, CONV_HALO)
    dw_rows = dw.reshape(CONV_WIDTH, ROW_BLOCKS, LANES)
    dwp = jnp.concatenate([dw_rows, dw_rows], axis=1).astype(BF16)
    tiled = lambda rows: pltpu.VMEM((rows // SUBLANES, ROW_BLOCKS * SUBLANES, LANES), F32)
    return pl.pallas_call(
        functools.partial(_conv_kernel, seq=s),
        grid=(b, s // CONV_ROW_TILE),
        in_specs=[prev, cur, nxt, _const_spec((1, d)), _const_spec((d, 2 * d)),
                  _const_spec((CONV_WIDTH, 2 * ROW_BLOCKS, LANES)), _const_spec((1, d)),
                  _const_spec((1, d)), _const_spec((1, d)), _const_spec((d, d))],
        out_specs=cur,
        out_shape=jax.ShapeDtypeStruct((b, s, d), F32),
        scratch_shapes=[tiled(CONV_EXT),
                        pltpu.VMEM((CONV_PAIRS, 2 * ROW_BLOCKS, LANES), BF16),
                        tiled(CONV_ROW_TILE)],
        compiler_params=_params(2),
        name="conv_module",
    )(x, x, x, g, w_glu, dwp, dw_bias, ln_g, ln_b, w_out)


def _dft_tables(seq):
    slab = seq // RADIX
    k1 = np.arange(slab, dtype=np.int64)[:, None]
    s1 = np.arange(slab, dtype=np.int64)[None, :]
    seq_tab = []
    for s2 in range(RADIX):
        ang = 2.0 * np.pi * ((k1 * (RADIX * s1 + s2)) % seq).astype(np.float64) / seq
        seq_tab.append(np.concatenate([np.cos(ang), -np.sin(ang)], axis=0) / np.sqrt(seq))
    c = np.arange(GROUP_DIM, dtype=np.int64)
    ang = 2.0 * np.pi * ((c[:, None] * c[None, :]) % GROUP_DIM).astype(np.float64) / GROUP_DIM
    chan = np.concatenate([np.cos(ang), np.sin(ang)], axis=0) / np.sqrt(GROUP_DIM)
    as_bf16 = lambda a: jnp.asarray(np.asarray(a, dtype=np.float32)).astype(BF16)
    return as_bf16(np.stack(seq_tab)), as_bf16(chan)


def _fourier_in_kernel(x_ref, g_ref, win_ref, o_ref, u_scr):
    h = _rms(x_ref[...], g_ref[...]).astype(BF16)
    u = jnp.dot(h, win_ref[...], preferred_element_type=F32)
    for c in range(ROW_BLOCKS):
        u_scr[c] = u[:, c * LANES:(c + 1) * LANES]
    for s2 in range(RADIX):
        for c in range(ROW_BLOCKS):
            o_ref[s2, :, c * LANES:(c + 1) * LANES] = (
                u_scr[c, pl.ds(s2, FOURIER_IN_TILE // RADIX, stride=RADIX), :].astype(BF16))


def _butterfly8(a):
    add = lambda p, q: (p[0] + q[0], p[1] + q[1])
    sub = lambda p, q: (p[0] - q[0], p[1] - q[1])
    sub_i = lambda p, q: (p[0] + q[1], p[1] - q[0])
    add_i = lambda p, q: (p[0] - q[1], p[1] + q[0])
    c = np.float32(np.sqrt(0.5))
    w1 = lambda p: ((p[0] + p[1]) * c, (p[1] - p[0]) * c)
    w3 = lambda p: ((p[1] - p[0]) * c, -((p[0] + p[1]) * c))
    p0, p1, p2, p3 = add(a[0], a[4]), sub(a[0], a[4]), add(a[2], a[6]), sub(a[2], a[6])
    q0, q1, q2, q3 = add(a[1], a[5]), sub(a[1], a[5]), add(a[3], a[7]), sub(a[3], a[7])
    e0, e2, e1, e3 = add(p0, p2), sub(p0, p2), sub_i(p1, p3), add_i(p1, p3)
    o0, o2, o1, o3 = add(q0, q2), sub(q0, q2), sub_i(q1, q3), add_i(q1, q3)
    o1, o3 = w1(o1), w3(o3)
    return [add(e0, o0), add(e1, o1), sub_i(e2, o2), add(e3, o3),
            sub(e0, o0), sub(e1, o1), add_i(e2, o2), sub(e3, o3)]


def _fourier_out_kernel(u_ref, tab_ref, chan_ref, x_ref, wout_ref, o_ref, z_scr, xri_scr, *, seq):
    slab = seq // RADIX
    chunks = slab // BFLY_ROWS
    pass_cols = FOURIER_PASS_GROUPS * GROUP_DIM
    j = pl.program_id(1)

    @pl.when(j == 0)
    def _():
        for ps in range(N_GROUPS // FOURIER_PASS_GROUPS):
            cols = slice(ps * pass_cols, (ps + 1) * pass_cols)
            for s2 in range(RADIX):
                z = jnp.dot(tab_ref[s2], u_ref[s2, :, cols], preferred_element_type=F32)
                z_scr[s2] = z.reshape(2, chunks, BFLY_ROWS, pass_cols)

            def bfly(t, carry):
                for lb in range(pass_cols // LANES):
                    lanes = slice(lb * LANES, (lb + 1) * LANES)
                    z = [(z_scr[s2, 0, t, :, lanes], z_scr[s2, 1, t, :, lanes]) for s2 in range(RADIX)]
                    out = _butterfly8(z)
                    gp, off = divmod(lb * LANES, GROUP_DIM)
                    gi = ps * FOURIER_PASS_GROUPS + gp
                    for k2 in range(RADIX):
                        xri_scr[k2, gi, t, :, off:off + LANES] = out[k2][0].astype(BF16)
                        xri_scr[k2, gi, t, :, GROUP_DIM + off:GROUP_DIM + off + LANES] = out[k2][1].astype(BF16)
                return carry

            lax.fori_loop(0, chunks, bfly, 0)

    slabs_per_tile = ROW_TILE // slab
    f = []
    for gi in range(N_GROUPS):
        xri = xri_scr[pl.ds(j * slabs_per_tile, slabs_per_tile), gi].reshape(ROW_TILE, 2 * GROUP_DIM)
        f.append(jnp.dot(xri, chan_ref[...], preferred_element_type=F32).astype(BF16))
    f = jnp.concatenate(f, axis=1)
    o_ref[...] = x_ref[...] + jnp.dot(f, wout_ref[...], preferred_element_type=F32)


def _fourier_mixer(x, g, w_in, w_out):
    b, s, d = x.shape
    slab = s // RADIX
    seq_tab, chan_tab = _dft_tables(s)
    u = pl.pallas_call(
        _fourier_in_kernel,
        grid=(b, s // FOURIER_IN_TILE),
        in_specs=[pl.BlockSpec((None, FOURIER_IN_TILE, d), lambda bi, i: (bi, i, 0)),
                  _const_spec((1, d)), _const_spec((d, d))],
        out_specs=pl.BlockSpec((None, RADIX, FOURIER_IN_TILE // RADIX, d), lambda bi, i: (bi, 0, i, 0)),
        out_shape=jax.ShapeDtypeStruct((b, RADIX, slab, d), BF16),
        scratch_shapes=[pltpu.VMEM((ROW_BLOCKS, FOURIER_IN_TILE, LANES), F32)],
        compiler_params=_params(2),
        name="fourier_in",
    )(x, g, w_in)
    pass_cols = FOURIER_PASS_GROUPS * GROUP_DIM
    row = pl.BlockSpec((None, ROW_TILE, d), lambda bi, i: (bi, i, 0))
    return pl.pallas_call(
        functools.partial(_fourier_out_kernel, seq=s),
        grid=(b, s // ROW_TILE),
        in_specs=[pl.BlockSpec((None, RADIX, slab, d), lambda bi, i: (bi, 0, 0, 0)),
                  _const_spec((RADIX, 2 * slab, slab)), _const_spec((2 * GROUP_DIM, GROUP_DIM)),
                  row, _const_spec((d, d))],
        out_specs=row,
        out_shape=jax.ShapeDtypeStruct((b, s, d), F32),
        scratch_shapes=[pltpu.VMEM((RADIX, 2, slab // BFLY_ROWS, BFLY_ROWS, pass_cols), F32),
                        pltpu.VMEM((RADIX, N_GROUPS, slab // BFLY_ROWS, BFLY_ROWS, 2 * GROUP_DIM), BF16)],
        compiler_params=pltpu.CompilerParams(dimension_semantics=("parallel", "arbitrary"),
                                             vmem_limit_bytes=VMEM_LIMIT),
        name="fourier_out",
    )(u, seq_tab, chan_tab, x, w_out)


def kernel(x, l0_norm_mix, l0_pool_w_in, l0_pool_w_group, l0_pool_scale, l0_pool_w_out, l0_norm_mlp, l0_mlp_up, l0_mlp_down, l1_norm_mix, l1_conv_w_in, l1_conv_dw, l1_conv_dw_bias, l1_conv_ln_g, l1_conv_ln_b, l1_conv_w_out, l1_norm_mlp, l1_mlp_up, l1_mlp_down, l2_norm_mix, l2_fourier_w_in, l2_fourier_w_out, l2_norm_mlp, l2_mlp_up, l2_mlp_down, l3_norm_mix, l3_pool_w_in, l3_pool_w_group, l3_pool_scale, l3_pool_w_out, l3_norm_mlp, l3_mlp_up, l3_mlp_down, final_norm):
    b, s, d = x.shape
    row = lambda a: a.astype(F32).reshape(1, -1)
    plain = lambda *ws: [(wgt.astype(F32), False) for wgt in ws]

    def mlp(x, g, up, down, final_g=None, cast=()):
        y, cast_out = _mlp(x.reshape(b * s, d), row(g), up, down,
                           None if final_g is None else row(final_g), cast)
        return y.reshape(b, s, d), cast_out

    x, (up0, down0) = _pool_mixer(x, row(l0_norm_mix), l0_pool_w_in.astype(BF16), l0_pool_w_group,
                                  l0_pool_scale, l0_pool_w_out, plain(l0_mlp_up, l0_mlp_down))
    x, (c_in, c_out, up1, down1) = mlp(
        x, l0_norm_mlp, up0, down0,
        cast=[(l1_conv_w_in.astype(F32), True)] + plain(l1_conv_w_out, l1_mlp_up, l1_mlp_down))
    x = _conv_module(x, row(l1_norm_mix), c_in, l1_conv_dw.astype(F32), row(l1_conv_dw_bias),
                     row(l1_conv_ln_g), row(l1_conv_ln_b), c_out)
    x, (f_in, f_out, up2, down2) = mlp(
        x, l1_norm_mlp, up1, down1,
        cast=plain(l2_fourier_w_in, l2_fourier_w_out, l2_mlp_up, l2_mlp_down))
    x = _fourier_mixer(x, row(l2_norm_mix), f_in, f_out)
    x, (p_in, up3, down3) = mlp(x, l2_norm_mlp, up2, down2, cast=plain(l3_pool_w_in, l3_mlp_up, l3_mlp_down))
    x, _ = _pool_mixer(x, row(l3_norm_mix), p_in, l3_pool_w_group, l3_pool_scale, l3_pool_w_out)
    x, _ = mlp(x, l3_norm_mlp, up3, down3, final_norm)
    return x
```

```python
import functools

import numpy as np
import jax
import jax.numpy as jnp
from jax import lax
from jax.experimental import pallas as pl
from jax.experimental.pallas import tpu as pltpu

D_MODEL = 1024
D_FF = 4 * D_MODEL
N_GROUPS = 4
GROUP_DIM = D_MODEL // N_GROUPS
POOL_WINDOWS = (2, 4, 8, 16)
CONV_WIDTH = 31
CONV_PAD = CONV_WIDTH // 2
NORM_EPS = 1e-6
LN_EPS = 1e-5

F32 = jnp.float32
BF16 = jnp.bfloat16

SUBLANES = 8
LANES = 128
ROW_BLOCKS = D_MODEL // LANES

ROW_TILE = 1024
CONV_ROW_TILE = 1024
MLP_ROW_TILE = 1024
POOL_ROW_TILE = 1024
FF_CHUNK = 1024
POOL_HALO = 8
CONV_HALO = 16
CONV_EXT = CONV_ROW_TILE + 2 * CONV_HALO
CONV_HALF = CONV_ROW_TILE // 2
CONV_PAIRS = CONV_HALF + 2 * CONV_HALO
RADIX = 8
BFLY_ROWS = 16
FOURIER_PASS_GROUPS = 2
FOURIER_IN_TILE = 1024
GLU_CHUNKS = 4
GLU_CHUNK_COLS = D_MODEL // GLU_CHUNKS
CONV_OUT_CHUNKS = 2
VMEM_LIMIT = 56 * 1024 * 1024


def _rms(x, g):
    ms = jnp.mean(x * x, axis=-1, keepdims=True)
    return x * lax.rsqrt(ms + NORM_EPS) * g


def _const_spec(shape):
    n = len(shape)
    return pl.BlockSpec(shape, lambda *_: (0,) * n, pipeline_mode=pl.Buffered(1))


def _params(n_axes):
    return pltpu.CompilerParams(dimension_semantics=("parallel",) * n_axes,
                                vmem_limit_bytes=VMEM_LIMIT)


def _cast_plumbing(weights, n_steps, index_map):
    in_specs, out_specs, out_shapes = [], [], []
    for wgt, _ in weights:
        rows, cols = wgt.shape
        block = (rows // n_steps, cols)
        in_specs.append(pl.BlockSpec(block, index_map))
        out_specs.append(pl.BlockSpec(block, index_map))
        out_shapes.append(jax.ShapeDtypeStruct(wgt.shape, BF16))
    return in_specs, out_specs, out_shapes


def _cast_chunks(in_refs, out_refs, glu_flags):
    for i_ref, o_ref, glu in zip(in_refs, out_refs, glu_flags):
        if glu:
            half = i_ref.shape[1] // 2
            for j in range(GLU_CHUNKS):
                src = slice(j * GLU_CHUNK_COLS, (j + 1) * GLU_CHUNK_COLS)
                o_ref[:, 2 * j * GLU_CHUNK_COLS:(2 * j + 1) * GLU_CHUNK_COLS] = i_ref[:, src].astype(BF16)
                o_ref[:, (2 * j + 1) * GLU_CHUNK_COLS:(2 * j + 2) * GLU_CHUNK_COLS] = (
                    i_ref[:, half + j * GLU_CHUNK_COLS:half + (j + 1) * GLU_CHUNK_COLS].astype(BF16))
        else:
            o_ref[...] = i_ref[...].astype(BF16)


def _mlp_kernel(*refs, final, glu_flags):
    n_cast = len(glu_flags)
    n_in = 4 + int(final)
    x_ref, g_ref, wup_ref, wdn_ref = refs[:4]
    cast_in = refs[n_in:n_in + n_cast]
    o_ref = refs[n_in + n_cast]
    cast_out = refs[n_in + n_cast + 1:]
    x = x_ref[...]
    h = _rms(x, g_ref[...]).astype(BF16)
    acc = x
    for c in range(D_FF // FF_CHUNK):
        cols = slice(c * FF_CHUNK, (c + 1) * FF_CHUNK)
        a = jnp.dot(h, wup_ref[:, cols], preferred_element_type=F32)
        a = jnp.maximum(a, 0.0)
        a = (a * a).astype(BF16)
        acc = acc + jnp.dot(a, wdn_ref[cols, :], preferred_element_type=F32)
    if final:
        acc = _rms(acc, refs[4][...])
    o_ref[...] = acc
    _cast_chunks(cast_in, cast_out, glu_flags)


def _mlp(x2, g, w_up, w_down, final_g=None, cast=()):
    n = x2.shape[0]
    final = final_g is not None
    n_steps = n // MLP_ROW_TILE
    row = pl.BlockSpec((MLP_ROW_TILE, D_MODEL), lambda i: (i, 0))
    in_specs = [row, _const_spec((1, D_MODEL)), _const_spec((D_MODEL, D_FF)),
                _const_spec((D_FF, D_MODEL))]
    args = [x2, g, w_up, w_down]
    if final:
        in_specs.append(_const_spec((1, D_MODEL)))
        args.append(final_g)
    c_in, c_out, c_shapes = _cast_plumbing(cast, n_steps, lambda i: (i, 0))
    outs = pl.pallas_call(
        functools.partial(_mlp_kernel, final=final, glu_flags=tuple(glu for _, glu in cast)),
        grid=(n_steps,),
        in_specs=in_specs + c_in,
        out_specs=[row] + c_out,
        out_shape=[jax.ShapeDtypeStruct((n, D_MODEL), F32)] + c_shapes,
        compiler_params=_params(1),
        name="mlp_final" if final else "mlp",
    )(*args, *[wgt for wgt, _ in cast])
    return outs[0], outs[1:]


def _halo_specs(seq, tile, halo):
    per_tile = tile // halo
    last = seq // halo - 1
    cur = pl.BlockSpec((None, tile, D_MODEL), lambda b, i: (b, i, 0))
    prev = pl.BlockSpec((None, halo, D_MODEL),
                        lambda b, i: (b, jnp.maximum(i * per_tile - 1, 0), 0))
    nxt = pl.BlockSpec((None, halo, D_MODEL),
                       lambda b, i: (b, jnp.minimum((i + 1) * per_tile, last), 0))
    return prev, cur, nxt


def _ext_positions(tile, halo):
    start = pl.program_id(1) * tile - halo
    return start + lax.broadcasted_iota(jnp.int32, (tile + 2 * halo, 1), 0)


def _window_total(u, w):
    n = u.shape[0]
    fwd = lambda a, k: pltpu.roll(a, n - k, axis=0)
    back = lambda a, k: pltpu.roll(a, k, axis=0)
    if w == 2:
        return u + back(u, 1)
    span = u + fwd(u, 1)
    width = 2
    while 2 * width < w:
        span = span + fwd(span, width)
        width *= 2
    return span + back(span, width)


def _pool_fold_kernel(wg_ref, sc_ref, wout_ref, o_ref):
    scaled = sc_ref[0] * wout_ref[...]
    o_ref[...] = jnp.dot(wg_ref[0], scaled, preferred_element_type=F32,
                         precision=lax.Precision.HIGHEST).astype(BF16)


def _pool_fold(w_group, scale, w_out):
    d = w_out.shape[1]
    return pl.pallas_call(
        _pool_fold_kernel,
        grid=(N_GROUPS,),
        in_specs=[pl.BlockSpec((1, GROUP_DIM, GROUP_DIM), lambda g: (g, 0, 0)),
                  pl.BlockSpec((1, GROUP_DIM, 1), lambda g: (g, 0, 0)),
                  pl.BlockSpec((GROUP_DIM, d), lambda g: (g, 0))],
        out_specs=pl.BlockSpec((GROUP_DIM, d), lambda g: (g, 0)),
        out_shape=jax.ShapeDtypeStruct((d, d), BF16),
        compiler_params=_params(1),
        name="pool_fold",
    )(w_group.astype(F32), scale.astype(F32).reshape(N_GROUPS, GROUP_DIM, 1), w_out.astype(F32))


def _pool_kernel(xp_ref, x_ref, xn_ref, g_ref, win_ref, weff_ref, *rest, seq, glu_flags):
    n_cast = len(glu_flags)
    cast_in, o_ref, cast_out = rest[:n_cast], rest[n_cast], rest[n_cast + 1:]
    tile = POOL_ROW_TILE
    x = x_ref[...]
    g = g_ref[...]
    h_ext = jnp.concatenate([_rms(xp_ref[...], g), _rms(x, g), _rms(xn_ref[...], g)], axis=0)
    u_ext = jnp.dot(h_ext.astype(BF16), win_ref[...], preferred_element_type=F32)
    pos_ext = _ext_positions(tile, POOL_HALO)
    valid = (pos_ext >= 0) & (pos_ext < seq)
    u_ext = jnp.where(valid, u_ext, 0.0)

    pos = pos_ext[POOL_HALO:POOL_HALO + tile]
    pooled = []
    for gi, w in enumerate(POOL_WINDOWS):
        half = w // 2
        u = u_ext[:, gi * GROUP_DIM:(gi + 1) * GROUP_DIM]
        tot = _window_total(u, w)[POOL_HALO:POOL_HALO + tile]
        cnt = jnp.minimum(pos + half, seq) - jnp.maximum(pos - half, 0)
        p = tot / cnt.astype(F32) - u[POOL_HALO:POOL_HALO + tile]
        pooled.append(p.astype(BF16))
    o_ref[...] = x + jnp.dot(jnp.concatenate(pooled, axis=1), weff_ref[...],
                             preferred_element_type=F32)
    _cast_chunks(cast_in, cast_out, glu_flags)


def _pool_mixer(x, g, w_in, w_group, scale, w_out, cast=()):
    b, s, d = x.shape
    tiles = s // POOL_ROW_TILE
    prev, cur, nxt = _halo_specs(s, POOL_ROW_TILE, POOL_HALO)
    w_eff = _pool_fold(w_group, scale, w_out)
    c_in, c_out, c_shapes = _cast_plumbing(cast, b * tiles, lambda bi, i: (bi * tiles + i, 0))
    outs = pl.pallas_call(
        functools.partial(_pool_kernel, seq=s, glu_flags=tuple(glu for _, glu in cast)),
        grid=(b, tiles),
        in_specs=[prev, cur, nxt, _const_spec((1, d)), _const_spec((d, d)), _const_spec((d, d))] + c_in,
        out_specs=[cur] + c_out,
        out_shape=[jax.ShapeDtypeStruct((b, s, d), F32)] + c_shapes,
        compiler_params=_params(2),
        name="pool_mixer",
    )(x, x, x, g, w_in, w_eff, *[wgt for wgt, _ in cast])
    return outs[0], outs[1:]


def _conv_kernel(xp_ref, x_ref, xn_ref, g_ref, win_ref, dwp_ref, dwb_ref, lng_ref, lnb_ref,
                 wout_ref, o_ref, a_ref, bp_ref, d_ref, *, seq):
    x = x_ref[...]
    g = g_ref[...]
    h_ext = jnp.concatenate([_rms(xp_ref[...], g), _rms(x, g), _rms(xn_ref[...], g)],
                            axis=0).astype(BF16)
    pos_ext = _ext_positions(CONV_ROW_TILE, CONV_HALO)
    valid = (pos_ext >= 0) & (pos_ext < seq)

    blocks_per_chunk = GLU_CHUNK_COLS // LANES
    for j in range(GLU_CHUNKS):
        vg = jnp.dot(h_ext, win_ref[:, j * 2 * GLU_CHUNK_COLS:(j + 1) * 2 * GLU_CHUNK_COLS],
                     preferred_element_type=F32)
        v = jnp.where(valid, vg[:, :GLU_CHUNK_COLS] * jax.nn.sigmoid(vg[:, GLU_CHUNK_COLS:]), 0.0)
        for cb in range(blocks_per_chunk):
            c = j * blocks_per_chunk + cb
            a_ref[:, c * SUBLANES:(c + 1) * SUBLANES, :] = (
                v[:, cb * LANES:(cb + 1) * LANES].reshape(CONV_EXT // SUBLANES, SUBLANES, LANES))

    def row_vreg(t, r):
        return (t, pl.ds(r, ROW_BLOCKS, stride=SUBLANES), slice(None))

    def pack(t, carry):
        for r in range(SUBLANES):
            lo = a_ref[row_vreg(t, r)]
            hi = a_ref[row_vreg(t + CONV_HALF // SUBLANES, r)]
            bp_ref[t * SUBLANES + r] = jnp.concatenate([lo, hi], axis=0).astype(BF16)
        return carry

    lax.fori_loop(0, CONV_PAIRS // SUBLANES, pack, 0, unroll=4)

    shift = CONV_HALO - CONV_PAD

    def conv(t, carry):
        base = t * SUBLANES + shift
        rows = [bp_ref[base + j] for j in range(SUBLANES + CONV_WIDTH - 1)]
        taps = [dwp_ref[k] for k in range(CONV_WIDTH)]
        for r in range(SUBLANES):
            acc = rows[r].astype(F32) * taps[0].astype(F32)
            for k in range(1, CONV_WIDTH):
                acc = acc + rows[r + k].astype(F32) * taps[k].astype(F32)
            d_ref[row_vreg(t, r)] = acc[:ROW_BLOCKS]
            d_ref[row_vreg(t + CONV_HALF // SUBLANES, r)] = acc[ROW_BLOCKS:]
        return carry

    lax.fori_loop(0, CONV_HALF // SUBLANES, conv, 0)

    rows = CONV_ROW_TILE // CONV_OUT_CHUNKS
    for ch in range(CONV_OUT_CHUNKS):
        tiles = slice(ch * rows // SUBLANES, (ch + 1) * rows // SUBLANES)
        c = jnp.concatenate(
            [d_ref[tiles, c * SUBLANES:(c + 1) * SUBLANES, :].reshape(rows, LANES) for c in range(ROW_BLOCKS)],
            axis=1) + dwb_ref[...]
        mu = jnp.mean(c, axis=-1, keepdims=True)
        cc = c - mu
        var = jnp.mean(cc * cc, axis=-1, keepdims=True)
        vn = cc * lax.rsqrt(var + LN_EPS) * lng_ref[...] + lnb_ref[...]
        act = (vn * jax.nn.sigmoid(vn)).astype(BF16)
        out_rows = slice(ch * rows, (ch + 1) * rows)
        o_ref[out_rows, :] = x_ref[out_rows, :] + jnp.dot(act, wout_ref[...], preferred_element_type=F32)


def _conv_module(x, g, w_glu, dw, dw_bias, ln_g, ln_b, w_out):
    b, s, d = x.shape
    prev, cur, nxt = _halo_specs(s, CONV_ROW_TILE, CONV_HALO)
    dw_rows = dw.reshape(CONV_WIDTH, ROW_BLOCKS, LANES)
    dwp = jnp.concatenate([dw_rows, dw_rows], axis=1).astype(BF16)
    tiled = lambda rows: pltpu.VMEM((rows // SUBLANES, ROW_BLOCKS * SUBLANES, LANES), F32)
    return pl.pallas_call(
        functools.partial(_conv_kernel, seq=s),
        grid=(b, s // CONV_ROW_TILE),
        in_specs=[prev, cur, nxt, _const_spec((1, d)), _const_spec((d, 2 * d)),
                  _const_spec((CONV_WIDTH, 2 * ROW_BLOCKS, LANES)), _const_spec((1, d)),
                  _const_spec((1, d)), _const_spec((1, d)), _const_spec((d, d))],
        out_specs=cur,
        out_shape=jax.ShapeDtypeStruct((b, s, d), F32),
        scratch_shapes=[tiled(CONV_EXT),
                        pltpu.VMEM((CONV_PAIRS, 2 * ROW_BLOCKS, LANES), BF16),
                        tiled(CONV_ROW_TILE)],
        compiler_params=_params(2),
        name="conv_module",
    )(x, x, x, g, w_glu, dwp, dw_bias, ln_g, ln_b, w_out)


def _dft_tables(seq):
    slab = seq // RADIX
    k1 = np.arange(slab, dtype=np.int64)[:, None]
    s1 = np.arange(slab, dtype=np.int64)[None, :]
    seq_tab = []
    for s2 in range(RADIX):
        ang = 2.0 * np.pi * ((k1 * (RADIX * s1 + s2)) % seq).astype(np.float64) / seq
        seq_tab.append(np.concatenate([np.cos(ang), -np.sin(ang)], axis=0) / np.sqrt(seq))
    c = np.arange(GROUP_DIM, dtype=np.int64)
    ang = 2.0 * np.pi * ((c[:, None] * c[None, :]) % GROUP_DIM).astype(np.float64) / GROUP_DIM
    chan = np.concatenate([np.cos(ang), np.sin(ang)], axis=0) / np.sqrt(GROUP_DIM)
    as_bf16 = lambda a: jnp.asarray(np.asarray(a, dtype=np.float32)).astype(BF16)
    return as_bf16(np.stack(seq_tab)), as_bf16(chan)


def _fourier_in_kernel(x_ref, g_ref, win_ref, o_ref, u_scr):
    h = _rms(x_ref[...], g_ref[...]).astype(BF16)
    u = jnp.dot(h, win_ref[...], preferred_element_type=F32)
    for c in range(ROW_BLOCKS):
        u_scr[c] = u[:, c * LANES:(c + 1) * LANES]
    for s2 in range(RADIX):
        for c in range(ROW_BLOCKS):
            o_ref[s2, :, c * LANES:(c + 1) * LANES] = (
                u_scr[c, pl.ds(s2, FOURIER_IN_TILE // RADIX, stride=RADIX), :].astype(BF16))


def _butterfly8(a):
    add = lambda p, q: (p[0] + q[0], p[1] + q[1])
    sub = lambda p, q: (p[0] - q[0], p[1] - q[1])
    sub_i = lambda p, q: (p[0] + q[1], p[1] - q[0])
    add_i = lambda p, q: (p[0] - q[1], p[1] + q[0])
    c = np.float32(np.sqrt(0.5))
    w1 = lambda p: ((p[0] + p[1]) * c, (p[1] - p[0]) * c)
    w3 = lambda p: ((p[1] - p[0]) * c, -((p[0] + p[1]) * c))
    p0, p1, p2, p3 = add(a[0], a[4]), sub(a[0], a[4]), add(a[2], a[6]), sub(a[2], a[6])
    q0, q1, q2, q3 = add(a[1], a[5]), sub(a[1], a[5]), add(a[3], a[7]), sub(a[3], a[7])
    e0, e2, e1, e3 = add(p0, p2), sub(p0, p2), sub_i(p1, p3), add_i(p1, p3)
    o0, o2, o1, o3 = add(q0, q2), sub(q0, q2), sub_i(q1, q3), add_i(q1, q3)
    o1, o3 = w1(o1), w3(o3)
    return [add(e0, o0), add(e1, o1), sub_i(e2, o2), add(e3, o3),
            sub(e0, o0), sub(e1, o1), add_i(e2, o2), sub(e3, o3)]


def _fourier_out_kernel(u_ref, tab_ref, chan_ref, x_ref, wout_ref, o_ref, z_scr, xri_scr, *, seq):
    slab = seq // RADIX
    chunks = slab // BFLY_ROWS
    pass_cols = FOURIER_PASS_GROUPS * GROUP_DIM
    j = pl.program_id(1)

    @pl.when(j == 0)
    def _():
        for ps in range(N_GROUPS // FOURIER_PASS_GROUPS):
            cols = slice(ps * pass_cols, (ps + 1) * pass_cols)
            for s2 in range(RADIX):
                z = jnp.dot(tab_ref[s2], u_ref[s2, :, cols], preferred_element_type=F32)
                z_scr[s2] = z.reshape(2, chunks, BFLY_ROWS, pass_cols)

            def bfly(t, carry):
                for lb in range(pass_cols // LANES):
                    lanes = slice(lb * LANES, (lb + 1) * LANES)
                    z = [(z_scr[s2, 0, t, :, lanes], z_scr[s2, 1, t, :, lanes]) for s2 in range(RADIX)]
                    out = _butterfly8(z)
                    gp, off = divmod(lb * LANES, GROUP_DIM)
                    gi = ps * FOURIER_PASS_GROUPS + gp
                    for k2 in range(RADIX):
                        xri_scr[k2, gi, t, :, off:off + LANES] = out[k2][0].astype(BF16)
                        xri_scr[k2, gi, t, :, GROUP_DIM + off:GROUP_DIM + off + LANES] = out[k2][1].astype(BF16)
                return carry

            lax.fori_loop(0, chunks, bfly, 0, unroll=2)

    slabs_per_tile = ROW_TILE // slab
    f = []
    for gi in range(N_GROUPS):
        xri = xri_scr[pl.ds(j * slabs_per_tile, slabs_per_tile), gi].reshape(ROW_TILE, 2 * GROUP_DIM)
        f.append(jnp.dot(xri, chan_ref[...], preferred_element_type=F32).astype(BF16))
    f = jnp.concatenate(f, axis=1)
    o_ref[...] = x_ref[...] + jnp.dot(f, wout_ref[...], preferred_element_type=F32)


def _fourier_mixer(x, g, w_in, w_out):
    b, s, d = x.shape
    slab = s // RADIX
    seq_tab, chan_tab = _dft_tables(s)
    u = pl.pallas_call(
        _fourier_in_kernel,
        grid=(b, s // FOURIER_IN_TILE),
        in_specs=[pl.BlockSpec((None, FOURIER_IN_TILE, d), lambda bi, i: (bi, i, 0)),
                  _const_spec((1, d)), _const_spec((d, d))],
        out_specs=pl.BlockSpec((None, RADIX, FOURIER_IN_TILE // RADIX, d), lambda bi, i: (bi, 0, i, 0)),
        out_shape=jax.ShapeDtypeStruct((b, RADIX, slab, d), BF16),
        scratch_shapes=[pltpu.VMEM((ROW_BLOCKS, FOURIER_IN_TILE, LANES), F32)],
        compiler_params=_params(2),
        name="fourier_in",
    )(x, g, w_in)
    pass_cols = FOURIER_PASS_GROUPS * GROUP_DIM
    row = pl.BlockSpec((None, ROW_TILE, d), lambda bi, i: (bi, i, 0))
    return pl.pallas_call(
        functools.partial(_fourier_out_kernel, seq=s),
        grid=(b, s // ROW_TILE),
        in_specs=[pl.BlockSpec((None, RADIX, slab, d), lambda bi, i: (bi, 0, 0, 0)),
                  _const_spec((RADIX, 2 * slab, slab)), _const_spec((2 * GROUP_DIM, GROUP_DIM)),
                  row, _const_spec((d, d))],
        out_specs=row,
        out_shape=jax.ShapeDtypeStruct((b, s, d), F32),
        scratch_shapes=[pltpu.VMEM((RADIX, 2, slab // BFLY_ROWS, BFLY_ROWS, pass_cols), F32),
                        pltpu.VMEM((RADIX, N_GROUPS, slab // BFLY_ROWS, BFLY_ROWS, 2 * GROUP_DIM), BF16)],
        compiler_params=pltpu.CompilerParams(dimension_semantics=("parallel", "arbitrary"),
                                             vmem_limit_bytes=VMEM_LIMIT),
        name="fourier_out",
    )(u, seq_tab, chan_tab, x, w_out)


def kernel(x, l0_norm_mix, l0_pool_w_in, l0_pool_w_group, l0_pool_scale, l0_pool_w_out, l0_norm_mlp, l0_mlp_up, l0_mlp_down, l1_norm_mix, l1_conv_w_in, l1_conv_dw, l1_conv_dw_bias, l1_conv_ln_g, l1_conv_ln_b, l1_conv_w_out, l1_norm_mlp, l1_mlp_up, l1_mlp_down, l2_norm_mix, l2_fourier_w_in, l2_fourier_w_out, l2_norm_mlp, l2_mlp_up, l2_mlp_down, l3_norm_mix, l3_pool_w_in, l3_pool_w_group, l3_pool_scale, l3_pool_w_out, l3_norm_mlp, l3_mlp_up, l3_mlp_down, final_norm):
    b, s, d = x.shape
    row = lambda a: a.astype(F32).reshape(1, -1)
    plain = lambda *ws: [(wgt.astype(F32), False) for wgt in ws]

    def mlp(x, g, up, down, final_g=None, cast=()):
        y, cast_out = _mlp(x.reshape(b * s, d), row(g), up, down,
                           None if final_g is None else row(final_g), cast)
        return y.reshape(b, s, d), cast_out

    x, (up0, down0) = _pool_mixer(x, row(l0_norm_mix), l0_pool_w_in.astype(BF16), l0_pool_w_group,
                                  l0_pool_scale, l0_pool_w_out, plain(l0_mlp_up, l0_mlp_down))
    x, (c_in, c_out, up1, down1) = mlp(
        x, l0_norm_mlp, up0, down0,
        cast=[(l1_conv_w_in.astype(F32), True)] + plain(l1_conv_w_out, l1_mlp_up, l1_mlp_down))
    x = _conv_module(x, row(l1_norm_mix), c_in, l1_conv_dw.astype(F32), row(l1_conv_dw_bias),
                     row(l1_conv_ln_g), row(l1_conv_ln_b), c_out)
    x, (f_in, f_out, up2, down2) = mlp(
        x, l1_norm_mlp, up1, down1,
        cast=plain(l2_fourier_w_in, l2_fourier_w_out, l2_mlp_up, l2_mlp_down))
    x = _fourier_mixer(x, row(l2_norm_mix), f_in, f_out)
    x, (p_in, up3, down3) = mlp(x, l2_norm_mlp, up2, down2, cast=plain(l3_pool_w_in, l3_mlp_up, l3_mlp_down))
    x, _ = _pool_mixer(x, row(l3_norm_mix), p_in, l3_pool_w_group, l3_pool_scale, l3_pool_w_out)
    x, _ = mlp(x, l3_norm_mlp, up3, down3, final_norm)
    return x
```
